```python
import math
import jax, jax.numpy as jnp
from jax import lax
import numpy as np

D_MODEL = 1024
BATCH = 4
SEQ = 4096
DEPTH = 1
DEC_BATCH = 128
DEC_SEQ = 8
PAST_LEN = 8192
PAGE_SIZE = 128

MIX_WIDTH = D_MODEL
H_G = 4
DV_G = MIX_WIDTH // 2 // H_G
DK_G = DV_G // 2
GLA_RANK = 16
GLA_GATE_NORM = 16.0
GLA_CHUNK = 16
H_D = 4
DV_D = MIX_WIDTH // 2 // H_D
DK_D = DV_D // 2
N_BUCKETS = 32
T5_MAX_EXACT = N_BUCKETS // 2
T5_MAX_DIST = 128
N_MEM = 256
H_X = 4
DH_X = D_MODEL // H_X
D_FF = 2816
ATTN_BLOCK = 128
EPS = 1e-6
NEG_INF = -1e30
IN_SIZES = (H_G * DK_G, H_G * DK_G, H_G * DV_G, H_G * DV_G, GLA_RANK, H_D * 2 * DK_D, H_D * 2 * DK_D, H_D * DV_D)
IN_COLS = sum(IN_SIZES)

kernel_name = 'hymba_gla_diffattn_macaron_step'

F32 = jnp.float32


def _rms(x, g):
    xf = x.astype(F32)
    y = xf * lax.rsqrt(jnp.mean(xf * xf, axis=-1, keepdims=True) + EPS)
    return y.astype(x.dtype) * g


def _half_ffn(x, g, wg, wu, wd):
    h = _rms(x, g)
    return 0.5 * ((jax.nn.silu(h @ wg) * (h @ wu)) @ wd)


def _t5_bias(q_pos, k_pos, table):
    n = jnp.maximum(q_pos[:, None] - k_pos[None, :], 0)
    large = T5_MAX_EXACT + (jnp.log(jnp.maximum(n, 1).astype(F32) / T5_MAX_EXACT)
                            / math.log(T5_MAX_DIST / T5_MAX_EXACT)
                            * (N_BUCKETS - T5_MAX_EXACT)).astype(jnp.int32)
    large = jnp.minimum(large, N_BUCKETS - 1)
    bucket = jnp.where(n < T5_MAX_EXACT, n, large)
    return table[bucket].transpose(2, 0, 1).astype(F32)


def _mixer_inputs(h, w_in, gla_w_gate_up, gla_b_gate, diff_q_norm, diff_k_norm):
    B, L, _ = h.shape
    idx = np.cumsum(IN_SIZES)[:-1].tolist()
    q_g, k_g, v_g, r_g, g_lr, q_d, k_d, v_d = jnp.split(h @ w_in, idx, axis=-1)
    q_g = q_g.reshape(B, L, H_G, DK_G).astype(F32) * (DK_G ** -0.5)
    k_g = k_g.reshape(B, L, H_G, DK_G).astype(F32)
    v_g = v_g.reshape(B, L, H_G, DV_G).astype(F32)
    logf = jax.nn.log_sigmoid((g_lr @ gla_w_gate_up + gla_b_gate).astype(F32)) / GLA_GATE_NORM
    logf = logf.reshape(B, L, H_G, DK_G)
    q_d = _rms(q_d.reshape(B, L, H_D, 2, DK_D), diff_q_norm) * (DK_D ** -0.5)
    k_d = _rms(k_d.reshape(B, L, H_D, 2, DK_D), diff_k_norm)
    v_d = v_d.reshape(B, L, H_D, DV_D)
    return q_g, k_g, v_g, r_g, logf, q_d, k_d, v_d


def _gla_chunked(q, k, v, logf, s0):
    B, L, H, DK = q.shape
    DV = v.shape[-1]
    C = math.gcd(L, GLA_CHUNK)
    n = L // C

    def chunks(t):
        return t.reshape(B, n, C, H, t.shape[-1]).transpose(1, 0, 3, 2, 4)

    tri = jnp.tril(jnp.ones((C, C), dtype=bool))

    def step(S, xs):
        qc, kc, vc, gc = xs
        b = jnp.cumsum(gc, axis=2)
        o_inter = jnp.einsum('bhtd,bhdv->bhtv', qc * jnp.exp(b), S)
        rel = b[:, :, :, None, :] - b[:, :, None, :, :]
        decay = jnp.exp(jnp.where(tri[:, :, None], rel, -jnp.inf))
        A = jnp.einsum('bhtd,bhsd,bhtsd->bhts', qc, kc, decay)
        o_intra = jnp.einsum('bhts,bhsv->bhtv', A, vc)
        b_last = b[:, :, -1:, :]
        S = (jnp.exp(b_last[:, :, 0, :])[..., None] * S
             + jnp.einsum('bhsd,bhsv->bhdv', kc * jnp.exp(b_last - b), vc))
        return S, o_inter + o_intra

    S, o = lax.scan(step, s0, (chunks(q), chunks(k), chunks(v), chunks(logf)))
    return o.transpose(1, 0, 3, 2, 4).reshape(B, L, H, DV), S


def _diff_attn_prompt(q, k, v, rel_bias):
    B, L, H, _, DK = q.shape
    DV = v.shape[-1]
    QB = min(ATTN_BLOCK, L)
    nqb = L // QB
    qb = q.reshape(B, nqb, QB, H, 2, DK).transpose(1, 0, 3, 4, 2, 5)
    k_pos = jnp.arange(L)

    def block(args):
        qi, bi = args
        q_pos = bi * QB + jnp.arange(QB)
        s = jnp.einsum('bhmqd,bkhmd->bhmqk', qi, k).astype(F32)
        s = s + _t5_bias(q_pos, k_pos, rel_bias)[None, :, None]
        s = jnp.where(k_pos[None, :] <= q_pos[:, None], s, NEG_INF)
        p = jax.nn.softmax(s, axis=-1)
        return jnp.einsum('bhmqk,bkhv->bhmqv', p, v)

    out = lax.map(block, (qb, jnp.arange(nqb)))
    return out.transpose(1, 0, 4, 2, 3, 5).reshape(B, L, H, 2, DV)


def _diff_attn_paged(q, k_new, v_new, k_pool, v_pool, page_table, rel_bias):
    Bd, Ld, H, _, DK = q.shape
    n_pages = page_table.shape[1]
    past_len = n_pages * PAGE_SIZE
    qh = q.transpose(0, 2, 3, 1, 4).astype(F32)
    q_pos = past_len + jnp.arange(Ld)
    s = jnp.einsum('bhmqd,bkhmd->bhmqk', qh, k_new).astype(F32)
    s = s + _t5_bias(q_pos, q_pos, rel_bias)[None, :, None]
    s = jnp.where(q_pos[None, :] <= q_pos[:, None], s, NEG_INF)
    m = jnp.max(s, axis=-1)
    p = jnp.exp(s - m[..., None])
    l = jnp.sum(p, axis=-1)
    acc = jnp.einsum('bhmqk,bkhv->bhmqv', p, v_new.astype(F32))

    def step(carry, pi):
        m, l, acc = carry
        phys = page_table[:, pi]
        kp = k_pool[phys]
        vp = v_pool[phys]
        k_pos = pi * PAGE_SIZE + jnp.arange(PAGE_SIZE)
        s = jnp.einsum('bhmqd,bkhmd->bhmqk', qh, kp).astype(F32)
        s = s + _t5_bias(q_pos, k_pos, rel_bias)[None, :, None]
        m_new = jnp.maximum(m, jnp.max(s, axis=-1))
        corr = jnp.exp(m - m_new)
        pr = jnp.exp(s - m_new[..., None])
        l = l * corr + jnp.sum(pr, axis=-1)
        acc = acc * corr[..., None] + jnp.einsum('bhmqk,bkhv->bhmqv', pr, vp.astype(F32))
        return (m_new, l, acc), None

    (m, l, acc), _ = lax.scan(step, (m, l, acc), jnp.arange(n_pages))
    o = acc / l[..., None]
    return o.transpose(0, 3, 1, 2, 4)


def _mixer_output(o_gla, r_g, o_diff, lam, lam_init, gla_out_norm, diff_subln, w_out, dtype):
    B, L = o_gla.shape[:2]
    og = _rms(o_gla, gla_out_norm).reshape(B, L, H_G * DV_G).astype(dtype) * jax.nn.silu(r_g)
    od = o_diff[..., 0, :] - lam * o_diff[..., 1, :]
    od = (_rms(od, diff_subln) * (1.0 - lam_init)).reshape(B, L, H_D * DV_D).astype(dtype)
    return jnp.concatenate([og, od], axis=-1) @ w_out


def _memory_kv(mem, mem_norm, wk, wv, k_norm):
    B, M, _ = mem.shape
    hm = _rms(mem, mem_norm)
    k = _rms((hm @ wk).reshape(B, M, H_X, DH_X), k_norm)
    v = (hm @ wv).reshape(B, M, H_X, DH_X)
    return k, v


def _cross_attn(h, mem_k, mem_v, wq, q_norm, wo):
    B, L, _ = h.shape
    q = _rms((h @ wq).reshape(B, L, H_X, DH_X), q_norm) * (DH_X ** -0.5)
    s = jnp.einsum('bqhd,bmhd->bhqm', q, mem_k).astype(F32)
    p = jax.nn.softmax(s, axis=-1).astype(mem_v.dtype)
    o = jnp.einsum('bhqm,bmhd->bqhd', p, mem_v).reshape(B, L, H_X * DH_X)
    return o @ wo


def setup_inputs(seed: int = 0) -> dict:
    key = jax.random.key(seed)
    ks = jax.random.split(key, 48)
    n_pages = PAST_LEN // PAGE_SIZE
    n_used = DEC_BATCH * n_pages
    n_pool = n_used + n_used // 4

    def nrm(i, shape, scale):
        return jax.random.normal(ks[i], shape, F32) * scale

    def gain(i, shape):
        return 1.0 + nrm(i, shape, 0.05)

    page_table = jax.random.permutation(ks[9], n_pool)[:n_used].reshape(DEC_BATCH, n_pages).astype(jnp.int32)
    Lr = DEPTH
    return {
        'x_prompt': nrm(0, (BATCH, SEQ, D_MODEL), 1.0),
        'x_sample': nrm(1, (DEC_BATCH, DEC_SEQ, D_MODEL), 1.0),
        'mem_prompt': nrm(2, (BATCH, N_MEM, D_MODEL), 1.0),
        'cache_diff_k': nrm(3, (Lr, n_pool, PAGE_SIZE, H_D, 2, DK_D), 1.0),
        'cache_diff_v': nrm(4, (Lr, n_pool, PAGE_SIZE, H_D, DV_D), 1.0),
        'state_gla': nrm(5, (Lr, DEC_BATCH, H_G, DK_G, DV_G), 0.3),
        'cache_mem_k': nrm(6, (Lr, DEC_BATCH, N_MEM, H_X, DH_X), 1.0),
        'cache_mem_v': nrm(7, (Lr, DEC_BATCH, N_MEM, H_X, DH_X), 1.0),
        'page_table': page_table,
        'rel_bias': nrm(8, (N_BUCKETS, H_D), 0.5),
        'ffn1_norm': gain(10, (Lr, D_MODEL)),
        'ffn1_w_gate': nrm(11, (Lr, D_MODEL, D_FF), D_MODEL ** -0.5),
        'ffn1_w_up': nrm(12, (Lr, D_MODEL, D_FF), D_MODEL ** -0.5),
        'ffn1_w_down': nrm(13, (Lr, D_FF, D_MODEL), D_FF ** -0.5),
        'mix_norm': gain(14, (Lr, D_MODEL)),
        'w_in': nrm(15, (Lr, D_MODEL, IN_COLS), D_MODEL ** -0.5),
        'gla_w_gate_up': nrm(16, (Lr, GLA_RANK, H_G * DK_G), GLA_RANK ** -0.5),
        'gla_b_gate': nrm(17, (Lr, H_G * DK_G), 0.1),
        'gla_out_norm': gain(18, (Lr, DV_G)),
        'diff_q_norm': gain(19, (Lr, DK_D)),
        'diff_k_norm': gain(20, (Lr, DK_D)),
        'diff_lam_q1': nrm(21, (Lr, DK_D), 0.1),
        'diff_lam_k1': nrm(22, (Lr, DK_D), 0.1),
        'diff_lam_q2': nrm(23, (Lr, DK_D), 0.1),
        'diff_lam_k2': nrm(24, (Lr, DK_D), 0.1),
        'diff_subln': gain(25, (Lr, DV_D)),
        'w_out': nrm(26, (Lr, MIX_WIDTH, D_MODEL), MIX_WIDTH ** -0.5),
        'xattn_norm': gain(27, (Lr, D_MODEL)),
        'mem_norm': gain(28, (Lr, D_MODEL)),
        'xattn_wq': nrm(29, (Lr, D_MODEL, H_X * DH_X), D_MODEL ** -0.5),
        'xattn_wk': nrm(30, (Lr, D_MODEL, H_X * DH_X), D_MODEL ** -0.5),
        'xattn_wv': nrm(31, (Lr, D_MODEL, H_X * DH_X), D_MODEL ** -0.5),
        'xattn_q_norm': gain(32, (Lr, DH_X)),
        'xattn_k_norm': gain(33, (Lr, DH_X)),
        'xattn_wo': nrm(34, (Lr, H_X * DH_X, D_MODEL), D_MODEL ** -0.5),
        'ffn2_norm': gain(35, (Lr, D_MODEL)),
        'ffn2_w_gate': nrm(36, (Lr, D_MODEL, D_FF), D_MODEL ** -0.5),
        'ffn2_w_up': nrm(37, (Lr, D_MODEL, D_FF), D_MODEL ** -0.5),
        'ffn2_w_down': nrm(38, (Lr, D_FF, D_MODEL), D_FF ** -0.5),
    }


def reference(x_prompt, x_sample, mem_prompt, cache_diff_k, cache_diff_v, state_gla, cache_mem_k, cache_mem_v,
              page_table, rel_bias, ffn1_norm, ffn1_w_gate, ffn1_w_up, ffn1_w_down, mix_norm, w_in,
              gla_w_gate_up, gla_b_gate, gla_out_norm, diff_q_norm, diff_k_norm, diff_lam_q1, diff_lam_k1,
              diff_lam_q2, diff_lam_k2, diff_subln, w_out, xattn_norm, mem_norm, xattn_wq, xattn_wk, xattn_wv,
              xattn_q_norm, xattn_k_norm, xattn_wo, ffn2_norm, ffn2_w_gate, ffn2_w_up, ffn2_w_down):
    xp, xs = x_prompt, x_sample
    kp_l, vp_l, ks_l, vs_l, sp_l, ss_l, mk_l, mv_l = [], [], [], [], [], [], [], []
    for l in range(DEPTH):
        lam_init = 0.8 - 0.6 * math.exp(-0.3 * l)
        lam = (jnp.exp(jnp.sum(diff_lam_q1[l].astype(F32) * diff_lam_k1[l].astype(F32)))
               - jnp.exp(jnp.sum(diff_lam_q2[l].astype(F32) * diff_lam_k2[l].astype(F32))) + lam_init)

        xp = xp + _half_ffn(xp, ffn1_norm[l], ffn1_w_gate[l], ffn1_w_up[l], ffn1_w_down[l])
        xs = xs + _half_ffn(xs, ffn1_norm[l], ffn1_w_gate[l], ffn1_w_up[l], ffn1_w_down[l])

        hp = _rms(xp, mix_norm[l])
        qg, kg, vg, rg, lf, qd, kd, vd = _mixer_inputs(hp, w_in[l], gla_w_gate_up[l], gla_b_gate[l],
                                                       diff_q_norm[l], diff_k_norm[l])
        s0 = jnp.zeros((xp.shape[0], H_G, DK_G, DV_G), F32)
        og, sp = _gla_chunked(qg, kg, vg, lf, s0)
        od = _diff_attn_prompt(qd, kd, vd, rel_bias)
        xp = xp + _mixer_output(og, rg, od, lam, lam_init, gla_out_norm[l], diff_subln[l], w_out[l], hp.dtype)
        kp_l.append(kd)
        vp_l.append(vd)
        sp_l.append(sp.astype(x_prompt.dtype))

        hs = _rms(xs, mix_norm[l])
        qg, kg, vg, rg, lf, qd, kd, vd = _mixer_inputs(hs, w_in[l], gla_w_gate_up[l], gla_b_gate[l],
                                                       diff_q_norm[l], diff_k_norm[l])
        og, ss = _gla_chunked(qg, kg, vg, lf, state_gla[l].astype(F32))
        od = _diff_attn_paged(qd, kd, vd, cache_diff_k[l], cache_diff_v[l], page_table, rel_bias)
        xs = xs + _mixer_output(og, rg, od, lam, lam_init, gla_out_norm[l], diff_subln[l], w_out[l], hs.dtype)
        ks_l.append(kd)
        vs_l.append(vd)
        ss_l.append(ss.astype(x_sample.dtype))

        mk, mv = _memory_kv(mem_prompt, mem_norm[l], xattn_wk[l], xattn_wv[l], xattn_k_norm[l])
        xp = xp + _cross_attn(_rms(xp, xattn_norm[l]), mk, mv, xattn_wq[l], xattn_q_norm[l], xattn_wo[l])
        xs = xs + _cross_attn(_rms(xs, xattn_norm[l]), cache_mem_k[l], cache_mem_v[l], xattn_wq[l],
                              xattn_q_norm[l], xattn_wo[l])
        mk_l.append(mk)
        mv_l.append(mv)

        xp = xp + _half_ffn(xp, ffn2_norm[l], ffn2_w_gate[l], ffn2_w_up[l], ffn2_w_down[l])
        xs = xs + _half_ffn(xs, ffn2_norm[l], ffn2_w_gate[l], ffn2_w_up[l], ffn2_w_down[l])

    return (xp, xs, jnp.stack(kp_l), jnp.stack(vp_l), jnp.stack(ks_l), jnp.stack(vs_l),
            jnp.stack(sp_l), jnp.stack(ss_l), jnp.stack(mk_l), jnp.stack(mv_l))
```

```python
import functools
import math

import numpy as np
import jax
import jax.numpy as jnp
from jax import lax
from jax.experimental import pallas as pl
from jax.experimental.pallas import tpu as pltpu

F32 = jnp.float32
BF16 = jnp.bfloat16

EPS = 1e-6
NEG = -1e30

H_G, DK_G, DV_G = 4, 64, 128
GLA_RANK = 16
GLA_GATE_NORM = 16.0
GLA_CHUNK = 16
H_D, DK_D, DV_D = 4, 64, 128
N_BUCKETS = 32
T5_MAX_EXACT = N_BUCKETS // 2
T5_MAX_DIST = 128
H_X = 4
PAGE = 128

VMEM_LIMIT = 56 * 1024 * 1024


def _cparams(sem):
    return pltpu.CompilerParams(dimension_semantics=sem, vmem_limit_bytes=VMEM_LIMIT)


def _rms_full(x, g):
    ms = jnp.mean(x * x, axis=-1, keepdims=True)
    return x * lax.rsqrt(ms + EPS) * g


def _idiv(x, d):
    if d & (d - 1) == 0:
        return lax.shift_right_logical(x, int(math.log2(d)))
    return x // d


def _nt_dot(a, b):
    return lax.dot_general(a, b, (((1,), (1,)), ((), ())), preferred_element_type=F32)


def _ffn_body(x_ref, g_ref, wg_ref, wu_ref, wd_ref, o_ref, h_scr, acc_scr):
    j = pl.program_id(1)

    @pl.when(j == 0)
    def _():
        h_scr[...] = _rms_full(x_ref[...], g_ref[...]).astype(BF16)
        acc_scr[...] = jnp.zeros_like(acc_scr)

    h = h_scr[...]
    gate = jnp.dot(h, wg_ref[...], preferred_element_type=F32)
    up = jnp.dot(h, wu_ref[...], preferred_element_type=F32)
    a = (gate * jax.nn.sigmoid(gate) * up).astype(BF16)
    acc_scr[...] += jnp.dot(a, wd_ref[...], preferred_element_type=F32)

    @pl.when(j == pl.num_programs(1) - 1)
    def _():
        o_ref[...] = x_ref[...] + 0.5 * acc_scr[...]


def _ffn(x, g, wg, wu, wd, *, tm, tf):
    n, d = x.shape
    dff = wg.shape[1]
    return pl.pallas_call(
        _ffn_body,
        grid=(n // tm, dff // tf),
        in_specs=[
            pl.BlockSpec((tm, d), lambda i, j: (i, 0)),
            pl.BlockSpec((1, d), lambda i, j: (0, 0)),
            pl.BlockSpec((d, tf), lambda i, j: (0, j)),
            pl.BlockSpec((d, tf), lambda i, j: (0, j)),
            pl.BlockSpec((tf, d), lambda i, j: (j, 0)),
        ],
        out_specs=pl.BlockSpec((tm, d), lambda i, j: (i, 0)),
        out_shape=jax.ShapeDtypeStruct((n, d), F32),
        scratch_shapes=[pltpu.VMEM((tm, d), BF16), pltpu.VMEM((tm, d), F32)],
        compiler_params=_cparams(("parallel", "arbitrary")),
        name="ffn",
    )(x, g, wg, wu, wd)


def _mixin_body(x_ref, g_ref, w_ref, wlr_ref, wup_ref, bg_ref, qn_ref, kn_ref, g64_ref,
                qg_ref, kg_ref, vg_ref, rg_ref, lf_ref, qd_ref, kd_ref, kdb_ref, vd_ref, vdb_ref):
    h = _rms_full(x_ref[...], g_ref[...]).astype(BF16)
    y = jnp.dot(h, w_ref[...], preferred_element_type=F32)
    qg_ref[...] = y[:, 0:256] * (DK_G ** -0.5)
    kg_ref[...] = y[:, 256:512]
    vg_ref[...] = y[:, 512:1024]
    rg_ref[...] = y[:, 1024:1536]
    qd = y[:, 1536:2048]
    kd = y[:, 2048:2560]
    vd = y[:, 2560:3072]

    glr = jnp.dot(h, wlr_ref[...], preferred_element_type=F32)
    z = jnp.dot(glr.astype(BF16), wup_ref[...], preferred_element_type=F32) + bg_ref[...]
    lf_ref[...] = (jnp.minimum(z, 0.0) - jnp.log(1.0 + jnp.exp(-jnp.abs(z)))) * (1.0 / GLA_GATE_NORM)

    def group_rms(t, gain):
        msq = jnp.dot((t * t).astype(BF16), g64_ref[...], preferred_element_type=F32) * (1.0 / DK_D)
        return t * lax.rsqrt(msq + EPS) * gain

    qdn = group_rms(qd, qn_ref[...]) * (DK_D ** -0.5)
    kdn = group_rms(kd, kn_ref[...])
    qd_ref[...] = qdn.astype(BF16)
    kd_ref[...] = kdn
    kdb_ref[...] = kdn.astype(BF16)
    vd_ref[...] = vd
    vdb_ref[...] = vd.astype(BF16)


def _mixin(x, g, w_main, w_lr, w_up, b_gate, qn, kn, g64, *, tm):
    n, d = x.shape
    full = lambda shape: pl.BlockSpec(shape, lambda i: (0, 0))
    row = lambda w: pl.BlockSpec((tm, w), lambda i: (i, 0))
    outs = [(256, F32), (256, F32), (512, F32), (512, F32), (256, F32),
            (512, BF16), (512, F32), (512, BF16), (512, F32), (512, BF16)]
    return pl.pallas_call(
        _mixin_body,
        grid=(n // tm,),
        in_specs=[row(d), full((1, d)), full(w_main.shape), full(w_lr.shape), full(w_up.shape),
                  full((1, 256)), full((1, 512)), full((1, 512)), full((512, 512))],
        out_specs=[row(w) for w, _ in outs],
        out_shape=[jax.ShapeDtypeStruct((n, w), dt) for w, dt in outs],
        compiler_params=_cparams(("parallel",)),
        name="mixer_in",
    )(x, g, w_main, w_lr, w_up, b_gate, qn, kn, g64)


def _gla_body(q_ref, k_ref, v_ref, lf_ref, s0_ref, tri_ref, g2_ref, o_ref, so_ref,
              st_scr, b_scr, *, C, n_chunks):
    tb = pl.program_id(1)

    @pl.when(tb == 0)
    def _():
        st_scr[...] = jnp.zeros_like(st_scr)
        for h in range(H_G):
            st_scr[h * DV_G:(h + 1) * DV_G, h * DK_G:(h + 1) * DK_G] = s0_ref[0, h]

    b_scr[...] = jnp.dot(tri_ref[...], lf_ref[0], preferred_element_type=F32,
                         precision=lax.Precision.HIGHEST)

    rows = lax.broadcasted_iota(jnp.int32, (H_G * DV_G, H_G * DK_G), 0)
    cols = lax.broadcasted_iota(jnp.int32, (H_G * DV_G, H_G * DK_G), 1)
    head_mask = _idiv(rows, DV_G) == _idiv(cols, DK_G)
    t_idx = lax.broadcasted_iota(jnp.int32, (C, H_G * DK_G), 0)

    def chunk(c, carry):
        r0 = pl.multiple_of(c * C, C)
        q = q_ref[0, pl.ds(r0, C), :]
        k = k_ref[0, pl.ds(r0, C), :]
        v = v_ref[0, pl.ds(r0, C), :]
        b = b_scr[pl.ds(r0, C), :]
        st = st_scr[...]

        o = _nt_dot((q * jnp.exp(b)).astype(BF16), st.astype(BF16))

        pieces = []
        for s in range(C):
            rel = jnp.where(t_idx >= s, b - b[s:s + 1, :], NEG)
            pieces.append(q * k[s:s + 1, :] * jnp.exp(rel))
        w = jnp.concatenate(pieces, axis=0).astype(BF16)
        a = jnp.dot(w, g2_ref[...], preferred_element_type=F32)
        for s in range(C):
            o = o + a[s * C:(s + 1) * C, :] * v[s:s + 1, :]
        o_ref[0, pl.ds(r0, C), :] = o

        b_last = b[C - 1:C, :]
        kt = (k * jnp.exp(b_last - b)).astype(BF16)
        upd = lax.dot_general(v.astype(BF16), kt, (((0,), (0,)), ((), ())),
                              preferred_element_type=F32)
        st_scr[...] = st * jnp.exp(b_last) + jnp.where(head_mask, upd, 0.0)
        return carry

    lax.fori_loop(0, n_chunks, chunk, 0)

    @pl.when(tb == pl.num_programs(1) - 1)
    def _():
        for h in range(H_G):
            so_ref[0, h] = st_scr[h * DV_G:(h + 1) * DV_G, h * DK_G:(h + 1) * DK_G]


def _gla(q, k, v, lf, s0t, *, tb):
    bsz, seq, _ = q.shape
    C = math.gcd(seq, GLA_CHUNK)
    tb = min(tb, seq)
    n_chunks = tb // C
    tri = np.zeros((tb, tb), np.float32)
    for c in range(n_chunks):
        tri[c * C:(c + 1) * C, c * C:(c + 1) * C] = np.tril(np.ones((C, C), np.float32))
    g2 = np.zeros((H_G * DK_G, H_G * DV_G), np.float32)
    for h in range(H_G):
        g2[h * DK_G:(h + 1) * DK_G, h * DV_G:(h + 1) * DV_G] = 1.0
    tok = lambda w: pl.BlockSpec((1, tb, w), lambda b, t: (b, t, 0))
    st_spec = pl.BlockSpec((1, H_G, DV_G, DK_G), lambda b, t: (b, 0, 0, 0))
    return pl.pallas_call(
        functools.partial(_gla_body, C=C, n_chunks=n_chunks),
        grid=(bsz, seq // tb),
        in_specs=[tok(256), tok(256), tok(512), tok(256), st_spec,
                  pl.BlockSpec((tb, tb), lambda b, t: (0, 0)),
                  pl.BlockSpec((256, 512), lambda b, t: (0, 0))],
        out_specs=[tok(512), st_spec],
        out_shape=[jax.ShapeDtypeStruct((bsz, seq, 512), F32),
                   jax.ShapeDtypeStruct((bsz, H_G, DV_G, DK_G), F32)],
        scratch_shapes=[pltpu.VMEM((H_G * DV_G, H_G * DK_G), F32), pltpu.VMEM((tb, 256), F32)],
        compiler_params=_cparams(("parallel", "arbitrary")),
        name="gla",
    )(q, k, v, lf, s0t, jnp.asarray(tri), jnp.asarray(g2, BF16))


def _dattn_body(q_ref, k_ref, v_ref, bias_ref, o_ref, m_scr, l_scr, acc_scr, *, T):
    qi = pl.program_id(2)
    ki = pl.program_id(3)

    @pl.when(ki == 0)
    def _():
        m_scr[...] = jnp.full_like(m_scr, NEG)
        l_scr[...] = jnp.zeros_like(l_scr)
        acc_scr[...] = jnp.zeros_like(acc_scr)

    @pl.when(ki <= qi)
    def _():
        q = q_ref[0]
        k = k_ref[0]
        v = v_ref[0]
        lane = lax.broadcasted_iota(jnp.int32, q.shape, 1)
        zero = jnp.zeros_like(q)
        bias = bias_ref[0, 0]
        s1 = _nt_dot(jnp.where(lane < DK_D, q, zero), k) + bias
        s2 = _nt_dot(jnp.where(lane >= DK_D, q, zero), k) + bias
        s = jnp.concatenate([s1, s2], axis=0)
        m_prev = m_scr[...]
        m_new = jnp.maximum(m_prev, jnp.max(s, axis=-1, keepdims=True))
        corr = jnp.exp(m_prev - m_new)
        p = jnp.exp(s - m_new)
        l_scr[...] = l_scr[...] * corr + jnp.sum(p, axis=-1, keepdims=True)
        acc_scr[...] = acc_scr[...] * corr + jnp.dot(p.astype(BF16), v, preferred_element_type=F32)
        m_scr[...] = m_new

    @pl.when(ki == qi)
    def _():
        o = acc_scr[...] / l_scr[...]
        o_ref[0, :, 0:DV_D] = o[:T]
        o_ref[0, :, DV_D:2 * DV_D] = o[T:]


def _dattn_prompt(q, k, v, bias_tiles, *, T):
    bsz, seq, _ = q.shape
    n = seq // T
    kv_spec = pl.BlockSpec((1, T, 128), lambda b, h, qi, ki: (b, jnp.minimum(ki, qi), h))

    def bias_map(b, h, qi, ki):
        return (h, jnp.clip(qi - ki, 0, 2), 0, 0)

    return pl.pallas_call(
        functools.partial(_dattn_body, T=T),
        grid=(bsz, H_D, n, n),
        in_specs=[pl.BlockSpec((1, T, 128), lambda b, h, qi, ki: (b, qi, h)), kv_spec, kv_spec,
                  pl.BlockSpec((1, 1, T, T), bias_map)],
        out_specs=pl.BlockSpec((1, T, 2 * DV_D), lambda b, h, qi, ki: (b, qi, h)),
        out_shape=jax.ShapeDtypeStruct((bsz, seq, H_D * 2 * DV_D), F32),
        scratch_shapes=[pltpu.VMEM((2 * T, 1), F32), pltpu.VMEM((2 * T, 1), F32),
                        pltpu.VMEM((2 * T, DV_D), F32)],
        compiler_params=_cparams(("parallel", "parallel", "parallel", "arbitrary")),
        name="diff_attn_prompt",
    )(q, k, v, bias_tiles)


def _paged_body(pt_ref, q_ref, kn_ref, vn_ref, bl_ref, bn_ref, *refs, P, LQ):
    k_refs = refs[:P]
    v_refs = refs[P:2 * P]
    o_ref = refs[2 * P]
    wq_scr, m_scr, l_scr, acc_scr = refs[2 * P + 1:]
    j = pl.program_id(1)
    R = H_D * 2 * LQ

    def update(s, vs):
        m_prev = m_scr[...]
        m_new = jnp.maximum(m_prev, jnp.max(s, axis=-1, keepdims=True))
        corr = jnp.exp(m_prev - m_new)
        p = jnp.exp(s - m_new)
        l_scr[...] = l_scr[...] * corr + jnp.sum(p, axis=-1, keepdims=True)
        pv = None
        for i, vp in enumerate(vs):
            t = jnp.dot(p[:, i * PAGE:(i + 1) * PAGE].astype(BF16), vp, preferred_element_type=F32)
            pv = t if pv is None else pv + t
        acc_scr[...] = acc_scr[...] * corr + pv
        m_scr[...] = m_new

    @pl.when(j == 0)
    def _():
        q = q_ref[0]
        qt = jnp.concatenate([q] * (H_D * 2), axis=0)
        rr = lax.broadcasted_iota(jnp.int32, qt.shape, 0)
        cc = lax.broadcasted_iota(jnp.int32, qt.shape, 1)
        wq_scr[...] = jnp.where(_idiv(rr, LQ) == _idiv(cc, DK_D), qt, jnp.zeros_like(qt))
        m_scr[...] = jnp.full_like(m_scr, NEG)
        l_scr[...] = jnp.zeros_like(l_scr)
        acc_scr[...] = jnp.zeros_like(acc_scr)
        pad = jnp.zeros((PAGE - LQ, H_D * 2 * DK_D), BF16)
        kn = jnp.concatenate([kn_ref[0], pad], axis=0)
        vn = jnp.concatenate([vn_ref[0], pad], axis=0)
        update(_nt_dot(wq_scr[...], kn) + bn_ref[...], [vn])

    wq = wq_scr[...]
    s_parts = [_nt_dot(wq, k_refs[i][0].astype(BF16)) for i in range(P)]
    is_last = j == pl.num_programs(1) - 1
    s_parts[P - 1] = s_parts[P - 1] + bl_ref[...] * jnp.where(is_last, 1.0, 0.0)
    update(jnp.concatenate(s_parts, axis=1), [v_refs[i][0].astype(BF16) for i in range(P)])

    @pl.when(is_last)
    def _():
        o = acc_scr[...] / l_scr[...]
        for h in range(H_D):
            o_ref[0, h * 2 * LQ:(h + 1) * 2 * LQ, :] = o[h * 2 * LQ:(h + 1) * 2 * LQ, h * DV_D:(h + 1) * DV_D]


def _dattn_paged(q, k_new, v_new, k_pool, v_pool, page_table, bias_last, bias_new, *, P):
    bd, lq, _ = q.shape
    n_pages = page_table.shape[1]
    R = H_D * 2 * lq
    tok = pl.BlockSpec((1, lq, 512), lambda b, j, pt: (b, 0, 0))
    tile = pl.BlockSpec((R, PAGE), lambda b, j, pt: (0, 0))

    def page_spec(i):
        return pl.BlockSpec((1, PAGE, 512), lambda b, j, pt: (pt[b * n_pages + j * P + i], 0, 0))

    grid_spec = pltpu.PrefetchScalarGridSpec(
        num_scalar_prefetch=1,
        grid=(bd, n_pages // P),
        in_specs=[tok, tok, tok, tile, tile] + [page_spec(i) for i in range(P)] * 2,
        out_specs=pl.BlockSpec((1, R, DV_D), lambda b, j, pt: (b, 0, 0)),
        scratch_shapes=[pltpu.VMEM((R, 512), BF16), pltpu.VMEM((R, 1), F32), pltpu.VMEM((R, 1), F32),
                        pltpu.VMEM((R, H_D * DV_D), F32)],
    )
    return pl.pallas_call(
        functools.partial(_paged_body, P=P, LQ=lq),
        grid_spec=grid_spec,
        out_shape=jax.ShapeDtypeStruct((bd, R, DV_D), F32),
        compiler_params=_cparams(("parallel", "arbitrary")),
        name="diff_attn_paged",
    )(page_table.reshape(-1), q, k_new, v_new, bias_last, bias_new,
      *([k_pool] * P), *([v_pool] * P))


def _mixout_body(x_ref, og_ref, rg_ref, od_ref, lam4_ref, gn_ref, sn_ref, wo_ref, o_ref, *, lam_init):
    lq1 = lam4_ref[0:1, :]
    lk1 = lam4_ref[1:2, :]
    lq2 = lam4_ref[2:3, :]
    lk2 = lam4_ref[3:4, :]
    lam = (jnp.exp(jnp.sum(lq1 * lk1, axis=-1, keepdims=True))
           - jnp.exp(jnp.sum(lq2 * lk2, axis=-1, keepdims=True)) + lam_init)
    og = og_ref[...]
    rg = rg_ref[...]
    od = od_ref[...]
    gn = gn_ref[...]
    sn = sn_ref[...]
    parts = []
    for h in range(H_G):
        t = og[:, h * DV_G:(h + 1) * DV_G]
        r = rg[:, h * DV_G:(h + 1) * DV_G]
        parts.append(_rms_full(t, gn) * (r * jax.nn.sigmoid(r)))
    for h in range(H_D):
        d = od[:, h * 2 * DV_D:h * 2 * DV_D + DV_D] - lam * od[:, h * 2 * DV_D + DV_D:(h + 1) * 2 * DV_D]
        parts.append(_rms_full(d, sn) * (1.0 - lam_init))
    cat = jnp.concatenate(parts, axis=-1).astype(BF16)
    o_ref[...] = x_ref[...] + jnp.dot(cat, wo_ref[...], preferred_element_type=F32)


def _mixout(x, og, rg, od, lam4, gn, sn, wo, *, tm, lam_init):
    n, d = x.shape
    full = lambda shape: pl.BlockSpec(shape, lambda i: (0, 0))
    row = lambda w: pl.BlockSpec((tm, w), lambda i: (i, 0))
    return pl.pallas_call(
        functools.partial(_mixout_body, lam_init=lam_init),
        grid=(n // tm,),
        in_specs=[row(d), row(512), row(512), row(1024), full((4, DK_D)), full((1, DV_G)),
                  full((1, DV_D)), full(wo.shape)],
        out_specs=row(d),
        out_shape=jax.ShapeDtypeStruct((n, d), F32),
        compiler_params=_cparams(("parallel",)),
        name="mixer_out",
    )(x, og, rg, od, lam4, gn, sn, wo)


def _proj_body(x_ref, g_ref, w_ref, hn_ref, *o_refs, head_norm, scale, dh):
    h = _rms_full(x_ref[...], g_ref[...]).astype(BF16)
    y = jnp.dot(h, w_ref[...], preferred_element_type=F32)
    if head_norm:
        hn = hn_ref[...]
        y = jnp.concatenate(
            [_rms_full(y[:, i * dh:(i + 1) * dh], hn) for i in range(y.shape[1] // dh)], axis=-1)
    y = y * scale
    for o_ref in o_refs:
        o_ref[...] = y.astype(o_ref.dtype)


def _proj(x, g, w, hn, *, tm, head_norm, scale, out_dtypes):
    n, d = x.shape
    dout = w.shape[1]
    dh = hn.shape[1]
    full = lambda shape: pl.BlockSpec(shape, lambda i: (0, 0))
    row = lambda wd: pl.BlockSpec((tm, wd), lambda i: (i, 0))
    return pl.pallas_call(
        functools.partial(_proj_body, head_norm=head_norm, scale=scale, dh=dh),
        grid=(n // tm,),
        in_specs=[row(d), full((1, d)), full(w.shape), full((1, dh))],
        out_specs=[row(dout) for _ in out_dtypes],
        out_shape=[jax.ShapeDtypeStruct((n, dout), dt) for dt in out_dtypes],
        compiler_params=_cparams(("parallel",)),
        name="proj",
    )(x, g, w, hn)


def _xattn_body(q_ref, k_ref, v_ref, o_ref, *, dh):
    q = q_ref[0]
    k = k_ref[0].astype(BF16)
    v = v_ref[0].astype(BF16)
    outs = []
    for h in range(H_X):
        sl = slice(h * dh, (h + 1) * dh)
        s = _nt_dot(q[:, sl], k[:, sl])
        m = jnp.max(s, axis=-1, keepdims=True)
        p = jnp.exp(s - m)
        p = p / jnp.sum(p, axis=-1, keepdims=True)
        outs.append(jnp.dot(p.astype(BF16), v[:, sl], preferred_element_type=F32))
    o_ref[0] = jnp.concatenate(outs, axis=-1).astype(o_ref.dtype)


def _xattn(q, k, v, *, tq):
    bsz, seq, d = q.shape
    m = k.shape[1]
    tq = min(tq, seq)
    kv = pl.BlockSpec((1, m, d), lambda b, i: (b, 0, 0))
    return pl.pallas_call(
        functools.partial(_xattn_body, dh=d // H_X),
        grid=(bsz, seq // tq),
        in_specs=[pl.BlockSpec((1, tq, d), lambda b, i: (b, i, 0)), kv, kv],
        out_specs=pl.BlockSpec((1, tq, d), lambda b, i: (b, i, 0)),
        out_shape=jax.ShapeDtypeStruct((bsz, seq, d), BF16),
        compiler_params=_cparams(("parallel", "parallel")),
        name="xattn",
    )(q, k, v)


def _resproj_body(x_ref, a_ref, w_ref, o_ref):
    o_ref[...] = x_ref[...] + jnp.dot(a_ref[...], w_ref[...], preferred_element_type=F32)


def _resproj(x, a, w, *, tm):
    n, d = x.shape
    row = lambda wd: pl.BlockSpec((tm, wd), lambda i: (i, 0))
    return pl.pallas_call(
        _resproj_body,
        grid=(n // tm,),
        in_specs=[row(d), row(a.shape[1]), pl.BlockSpec(w.shape, lambda i: (0, 0))],
        out_specs=row(d),
        out_shape=jax.ShapeDtypeStruct((n, d), F32),
        compiler_params=_cparams(("parallel",)),
        name="resproj",
    )(x, a, w)


def _bias_by_distance(table, n):
    large = T5_MAX_EXACT + (jnp.log(jnp.maximum(n, 1).astype(F32) / T5_MAX_EXACT)
                            / math.log(T5_MAX_DIST / T5_MAX_EXACT)
                            * (N_BUCKETS - T5_MAX_EXACT)).astype(jnp.int32)
    large = jnp.minimum(large, N_BUCKETS - 1)
    bucket = jnp.where(n < T5_MAX_EXACT, n, large)
    return table[bucket].T.astype(F32)


def kernel(x_prompt, x_sample, mem_prompt, cache_diff_k, cache_diff_v, state_gla, cache_mem_k, cache_mem_v, page_table, rel_bias, ffn1_norm, ffn1_w_gate, ffn1_w_up, ffn1_w_down, mix_norm, w_in, gla_w_gate_up, gla_b_gate, gla_out_norm, diff_q_norm, diff_k_norm, diff_lam_q1, diff_lam_k1, diff_lam_q2, diff_lam_k2, diff_subln, w_out, xattn_norm, mem_norm, xattn_wq, xattn_wk, xattn_wv, xattn_q_norm, xattn_k_norm, xattn_wo, ffn2_norm, ffn2_w_gate, ffn2_w_up, ffn2_w_down):
    depth = ffn1_norm.shape[0]
    B, L, D = x_prompt.shape
    Bd, Ld, _ = x_sample.shape
    M = mem_prompt.shape[1]
    n_pool = cache_diff_k.shape[1]
    past_len = page_table.shape[1] * PAGE
    T = 512
    TM = 512
    P_PAGES = 8

    xp = x_prompt.reshape(B * L, D)
    xs = x_sample.reshape(Bd * Ld, D)
    outs = {k: [] for k in ("kp", "vp", "ks", "vs", "sp", "ss", "mk", "mv")}

    g64 = jnp.asarray(np.kron(np.eye(512 // DK_D, dtype=np.float32), np.ones((DK_D, DK_D), np.float32)), BF16)

    for l in range(depth):
        lam_init = 0.8 - 0.6 * math.exp(-0.3 * l)
        row = lambda a: a[l].reshape(1, -1)

        w1g, w1u, w1d = (w[l].astype(BF16) for w in (ffn1_w_gate, ffn1_w_up, ffn1_w_down))
        w2g, w2u, w2d = (w[l].astype(BF16) for w in (ffn2_w_gate, ffn2_w_up, ffn2_w_down))
        wi = w_in[l]
        lr0 = 2 * H_G * DK_G + 2 * H_G * DV_G
        w_main = jnp.concatenate([wi[:, :lr0], wi[:, lr0 + GLA_RANK:]], axis=1).astype(BF16)
        w_lr = jnp.pad(wi[:, lr0:lr0 + GLA_RANK], ((0, 0), (0, 128 - GLA_RANK))).astype(BF16)
        w_up = jnp.pad(gla_w_gate_up[l], ((0, 128 - GLA_RANK), (0, 0))).astype(BF16)
        qn = jnp.tile(diff_q_norm[l], 2 * H_D).reshape(1, -1)
        kn = jnp.tile(diff_k_norm[l], 2 * H_D).reshape(1, -1)
        lam4 = jnp.stack([diff_lam_q1[l], diff_lam_k1[l], diff_lam_q2[l], diff_lam_k2[l]]).astype(F32)
        wo_mix = w_out[l].astype(BF16)
        wq_x, wk_x, wv_x, wo_x = (w[l].astype(BF16) for w in (xattn_wq, xattn_wk, xattn_wv, xattn_wo))

        nd = jnp.arange(2 * T + PAGE + Ld)
        bias_d = _bias_by_distance(rel_bias, nd)
        far = bias_d[:, -1][:, None, None]
        r = jnp.arange(T)[:, None]
        c = jnp.arange(T)[None, :]
        tile_diag = jnp.where(c <= r, bias_d[:, jnp.maximum(r - c, 0)] - far, NEG)
        tile_sub = bias_d[:, T + r - c] - far
        bias_tiles = jnp.stack([tile_diag, tile_sub, jnp.zeros_like(tile_sub)], axis=1)
        qi = jnp.arange(Ld)[:, None]
        cp = jnp.arange(PAGE)[None, :]
        last = bias_d[:, PAGE + qi - cp] - far
        new = jnp.where(cp <= qi, bias_d[:, jnp.maximum(qi - cp, 0)] - far, NEG)
        rows = lambda t: jnp.broadcast_to(t[:, None], (H_D, 2, Ld, PAGE)).reshape(H_D * 2 * Ld, PAGE)
        bias_last, bias_new = rows(last), rows(new)

        xp = _ffn(xp, row(ffn1_norm), w1g, w1u, w1d, tm=1024, tf=256)
        xs = _ffn(xs, row(ffn1_norm), w1g, w1u, w1d, tm=1024, tf=256)

        mix_args = (row(mix_norm), w_main, w_lr, w_up, row(gla_b_gate), qn, kn, g64)
        qg, kg, vg, rg, lf, qd, kd, kdb, vd, vdb = _mixin(xp, *mix_args, tm=TM)
        b3 = lambda a: a.reshape(B, L, -1)
        s0 = jnp.zeros((B, H_G, DV_G, DK_G), F32)
        og, spt = _gla(b3(qg), b3(kg), b3(vg), b3(lf), s0, tb=256)
        od = _dattn_prompt(b3(qd), b3(kdb), b3(vdb), bias_tiles, T=T)
        xp = _mixout(xp, og.reshape(B * L, -1), rg, od.reshape(B * L, -1), lam4, row(gla_out_norm),
                     row(diff_subln), wo_mix, tm=TM, lam_init=lam_init)
        outs["kp"].append(kd.reshape(B, L, H_D, 2, DK_D))
        outs["vp"].append(vd.reshape(B, L, H_D, DV_D))
        outs["sp"].append(spt.transpose(0, 1, 3, 2))

        qg, kg, vg, rg, lf, qd, kd, kdb, vd, vdb = _mixin(xs, *mix_args, tm=TM)
        s3 = lambda a: a.reshape(Bd, Ld, -1)
        og, sst = _gla(s3(qg), s3(kg), s3(vg), s3(lf), state_gla[l].transpose(0, 1, 3, 2), tb=Ld)
        k_pool = cache_diff_k[l].reshape(n_pool, PAGE, H_D * 2 * DK_D)
        v_pool = cache_diff_v[l].reshape(n_pool, PAGE, H_D * DV_D)
        od = _dattn_paged(s3(qd), s3(kdb), s3(vdb), k_pool, v_pool, page_table, bias_last, bias_new,
                          P=P_PAGES)
        od = od.reshape(Bd, H_D, 2, Ld, DV_D).transpose(0, 3, 1, 2, 4).reshape(Bd * Ld, -1)
        xs = _mixout(xs, og.reshape(Bd * Ld, -1), rg, od, lam4, row(gla_out_norm), row(diff_subln),
                     wo_mix, tm=TM, lam_init=lam_init)
        outs["ks"].append(kd.reshape(Bd, Ld, H_D, 2, DK_D))
        outs["vs"].append(vd.reshape(Bd, Ld, H_D, DV_D))
        outs["ss"].append(sst.transpose(0, 1, 3, 2))

        dh = D // H_X
        mem2 = mem_prompt.reshape(B * M, D)
        (mk,) = _proj(mem2, row(mem_norm), wk_x, row(xattn_k_norm), tm=TM, head_norm=True, scale=1.0,
                      out_dtypes=(F32,))
        ones_dh = jnp.ones((1, dh), F32)
        (mv,) = _proj(mem2, row(mem_norm), wv_x, ones_dh, tm=TM, head_norm=False, scale=1.0,
                      out_dtypes=(F32,))
        (qp,) = _proj(xp, row(xattn_norm), wq_x, row(xattn_q_norm), tm=TM, head_norm=True,
                      scale=dh ** -0.5, out_dtypes=(BF16,))
        (qs,) = _proj(xs, row(xattn_norm), wq_x, row(xattn_q_norm), tm=TM, head_norm=True,
                      scale=dh ** -0.5, out_dtypes=(BF16,))
        op = _xattn(qp.reshape(B, L, D), mk.reshape(B, M, D), mv.reshape(B, M, D), tq=512)
        osm = _xattn(qs.reshape(Bd, Ld, D), cache_mem_k[l].reshape(Bd, M, D),
                     cache_mem_v[l].reshape(Bd, M, D), tq=Ld)
        xp = _resproj(xp, op.reshape(B * L, D), wo_x, tm=TM)
        xs = _resproj(xs, osm.reshape(Bd * Ld, D), wo_x, tm=TM)
        outs["mk"].append(mk.reshape(B, M, H_X, dh))
        outs["mv"].append(mv.reshape(B, M, H_X, dh))

        xp = _ffn(xp, row(ffn2_norm), w2g, w2u, w2d, tm=1024, tf=256)
        xs = _ffn(xs, row(ffn2_norm), w2g, w2u, w2d, tm=1024, tf=256)

    st = lambda key: jnp.stack(outs[key])
    return (xp.reshape(B, L, D), xs.reshape(Bd, Ld, D), st("kp"), st("vp"), st("ks"), st("vs"),
            st("sp"), st("ss"), st("mk"), st("mv"))
```

```python
import functools
import math

import numpy as np
import jax
import jax.numpy as jnp
from jax import lax
from jax.experimental import pallas as pl
from jax.experimental.pallas import tpu as pltpu

F32 = jnp.float32
BF16 = jnp.bfloat16

EPS = 1e-6
NEG = -1e30
LOG2E = 1.0 / math.log(2.0)

H_G, DK_G, DV_G = 4, 64, 128
GLA_RANK = 16
GLA_GATE_NORM = 16.0
GLA_CHUNK = 16
H_D, DK_D, DV_D = 4, 64, 128
N_BUCKETS = 32
T5_MAX_EXACT = N_BUCKETS // 2
T5_MAX_DIST = 128
H_X = 4
PAGE = 128
LANES = 128

VMEM_LIMIT = 56 * 1024 * 1024


def _cparams(sem):
    return pltpu.CompilerParams(dimension_semantics=sem, vmem_limit_bytes=VMEM_LIMIT)


def _rms_full(x, g):
    ms = jnp.mean(x * x, axis=-1, keepdims=True)
    return x * lax.rsqrt(ms + EPS) * g


def _idiv(x, d):
    if d & (d - 1) == 0:
        return lax.shift_right_logical(x, int(math.log2(d)))
    return x // d


def _nt_dot(a, b):
    return lax.dot_general(a, b, (((1,), (1,)), ((), ())), preferred_element_type=F32)


def _ffn_body(x_ref, g_ref, wg_ref, wu_ref, wd_ref, o_ref, h_scr, acc_scr):
    j = pl.program_id(1)

    @pl.when(j == 0)
    def _():
        h_scr[...] = _rms_full(x_ref[...], g_ref[...]).astype(BF16)
        acc_scr[...] = jnp.zeros_like(acc_scr)

    h = h_scr[...]
    gate = jnp.dot(h, wg_ref[...], preferred_element_type=F32)
    up = jnp.dot(h, wu_ref[...], preferred_element_type=F32)
    a = (gate * jax.nn.sigmoid(gate) * up).astype(BF16)
    acc_scr[...] += jnp.dot(a, wd_ref[...], preferred_element_type=F32)

    @pl.when(j == pl.num_programs(1) - 1)
    def _():
        o_ref[...] = x_ref[...] + 0.5 * acc_scr[...]


def _ffn(x, g, wg, wu, wd, *, tm, tf):
    n, d = x.shape
    dff = wg.shape[1]
    return pl.pallas_call(
        _ffn_body,
        grid=(n // tm, dff // tf),
        in_specs=[
            pl.BlockSpec((tm, d), lambda i, j: (i, 0)),
            pl.BlockSpec((1, d), lambda i, j: (0, 0)),
            pl.BlockSpec((d, tf), lambda i, j: (0, j)),
            pl.BlockSpec((d, tf), lambda i, j: (0, j)),
            pl.BlockSpec((tf, d), lambda i, j: (j, 0)),
        ],
        out_specs=pl.BlockSpec((tm, d), lambda i, j: (i, 0)),
        out_shape=jax.ShapeDtypeStruct((n, d), F32),
        scratch_shapes=[pltpu.VMEM((tm, d), BF16), pltpu.VMEM((tm, d), F32)],
        compiler_params=_cparams(("parallel", "arbitrary")),
        name="ffn",
    )(x, g, wg, wu, wd)


def _mixin_body(x_ref, g_ref, w_ref, wlr_ref, wup_ref, bg_ref, qn_ref, kn_ref, g64_ref,
                qg_ref, kg_ref, vg_ref, rg_ref, lf_ref, qd_ref, kd_ref, kdb_ref, vd_ref, vdb_ref,
                *, k_transposed):
    h = _rms_full(x_ref[...], g_ref[...]).astype(BF16)
    y = jnp.dot(h, w_ref[...], preferred_element_type=F32)
    qg_ref[...] = y[:, 0:256] * (DK_G ** -0.5)
    kg_ref[...] = y[:, 256:512]
    vg_ref[...] = y[:, 512:1024]
    rg_ref[...] = y[:, 1024:1536]
    qd = y[:, 1536:2048]
    kd = y[:, 2048:2560]
    vd = y[:, 2560:3072]

    glr = jnp.dot(h, wlr_ref[...], preferred_element_type=F32)
    z = jnp.dot(glr.astype(BF16), wup_ref[...], preferred_element_type=F32) + bg_ref[...]
    lf_ref[...] = (jnp.minimum(z, 0.0) - jnp.log(1.0 + jnp.exp(-jnp.abs(z)))) * (1.0 / GLA_GATE_NORM)

    def group_rms(t, gain):
        msq = jnp.dot((t * t).astype(BF16), g64_ref[...], preferred_element_type=F32) * (1.0 / DK_D)
        return t * lax.rsqrt(msq + EPS) * gain

    qdn = group_rms(qd, qn_ref[...]) * (DK_D ** -0.5 * LOG2E)
    kdn = group_rms(kd, kn_ref[...])
    qd_ref[...] = qdn.astype(BF16)
    if k_transposed:
        kd_ref[0] = kdn.T
    else:
        kd_ref[...] = kdn
    kdb_ref[...] = kdn.astype(BF16)
    for h in range(H_D):
        vd_ref[pl.ds(h, vd.shape[0], stride=H_D), :] = vd[:, h * DV_D:(h + 1) * DV_D]
    vdb_ref[...] = vd.astype(BF16)


def _mixin(x, g, w_main, w_lr, w_up, b_gate, qn, kn, g64, *, tm, seq=None):
    n, d = x.shape
    full = lambda shape: pl.BlockSpec(shape, lambda i: (0, 0))
    row = lambda w: pl.BlockSpec((tm, w), lambda i: (i, 0))
    outs = [(256, F32), (256, F32), (512, F32), (512, F32), (256, F32),
            (512, BF16), (512, F32), (512, BF16), (512, F32), (512, BF16)]
    out_specs = [row(w) for w, _ in outs]
    out_shape = [jax.ShapeDtypeStruct((n, w), dt) for w, dt in outs]
    if seq is not None:
        nt = seq // tm
        out_specs[6] = pl.BlockSpec((1, 512, tm), lambda i: (i // nt, 0, i % nt))
        out_shape[6] = jax.ShapeDtypeStruct((n // seq, 512, seq), F32)
    out_specs[8] = pl.BlockSpec((tm * H_D, DV_D), lambda i: (i, 0))
    out_shape[8] = jax.ShapeDtypeStruct((n * H_D, DV_D), F32)
    return pl.pallas_call(
        functools.partial(_mixin_body, k_transposed=seq is not None),
        grid=(n // tm,),
        in_specs=[row(d), full((1, d)), full(w_main.shape), full(w_lr.shape), full(w_up.shape),
                  full((1, 256)), full((1, 512)), full((1, 512)), full((512, 512))],
        out_specs=out_specs,
        out_shape=out_shape,
        compiler_params=_cparams(("parallel",)),
        name="mixer_in",
    )(x, g, w_main, w_lr, w_up, b_gate, qn, kn, g64)


def _gla_body(q_ref, k_ref, v_ref, lf_ref, s0_ref, tri_ref, g2_ref, o_ref, so_ref,
              st_scr, nat_scr, b_scr, *, C, n_chunks):
    tb = pl.program_id(1)

    @pl.when(tb == 0)
    def _():
        nat_scr[...] = jnp.zeros_like(nat_scr)
        for h in range(H_G):
            nat_scr[h * DK_G:(h + 1) * DK_G, h * DV_G:(h + 1) * DV_G] = s0_ref[0, h]
        st_scr[...] = nat_scr[...].T

    b_scr[...] = jnp.dot(tri_ref[...], lf_ref[0], preferred_element_type=F32,
                         precision=lax.Precision.HIGHEST)

    rows = lax.broadcasted_iota(jnp.int32, (H_G * DV_G, H_G * DK_G), 0)
    cols = lax.broadcasted_iota(jnp.int32, (H_G * DV_G, H_G * DK_G), 1)
    head_mask = _idiv(rows, DV_G) == _idiv(cols, DK_G)
    t_idx = lax.broadcasted_iota(jnp.int32, (C, H_G * DK_G), 0)

    def chunk(c, carry):
        r0 = pl.multiple_of(c * C, C)
        q = q_ref[0, pl.ds(r0, C), :]
        k = k_ref[0, pl.ds(r0, C), :]
        v = v_ref[0, pl.ds(r0, C), :]
        b = b_scr[pl.ds(r0, C), :]
        st = st_scr[...]

        o = _nt_dot((q * jnp.exp(b)).astype(BF16), st.astype(BF16))

        pieces = []
        for s in range(C):
            rel = jnp.where(t_idx >= s, b - b[s:s + 1, :], NEG)
            pieces.append(q * k[s:s + 1, :] * jnp.exp(rel))
        w = jnp.concatenate(pieces, axis=0).astype(BF16)
        a = jnp.dot(w, g2_ref[...], preferred_element_type=F32)
        for s in range(C):
            o = o + a[s * C:(s + 1) * C, :] * v[s:s + 1, :]
        o_ref[0, pl.ds(r0, C), :] = o

        b_last = b[C - 1:C, :]
        kt = (k * jnp.exp(b_last - b)).astype(BF16)
        upd = lax.dot_general(v.astype(BF16), kt, (((0,), (0,)), ((), ())),
                              preferred_element_type=F32)
        st_scr[...] = st * jnp.exp(b_last) + jnp.where(head_mask, upd, 0.0)
        return carry

    lax.fori_loop(0, n_chunks, chunk, 0)

    @pl.when(tb == pl.num_programs(1) - 1)
    def _():
        nat = st_scr[...].T
        for h in range(H_G):
            so_ref[0, h] = nat[h * DK_G:(h + 1) * DK_G, h * DV_G:(h + 1) * DV_G]


def _gla(q, k, v, lf, s0, *, tb):
    bsz, seq, _ = q.shape
    C = math.gcd(seq, GLA_CHUNK)
    tb = min(tb, seq)
    n_chunks = tb // C
    tri = np.zeros((tb, tb), np.float32)
    for c in range(n_chunks):
        tri[c * C:(c + 1) * C, c * C:(c + 1) * C] = np.tril(np.ones((C, C), np.float32))
    g2 = np.zeros((H_G * DK_G, H_G * DV_G), np.float32)
    for h in range(H_G):
        g2[h * DK_G:(h + 1) * DK_G, h * DV_G:(h + 1) * DV_G] = 1.0
    tok = lambda w: pl.BlockSpec((1, tb, w), lambda b, t: (b, t, 0))
    st_spec = pl.BlockSpec((1, H_G, DK_G, DV_G), lambda b, t: (b, 0, 0, 0))
    return pl.pallas_call(
        functools.partial(_gla_body, C=C, n_chunks=n_chunks),
        grid=(bsz, seq // tb),
        in_specs=[tok(256), tok(256), tok(512), tok(256), st_spec,
                  pl.BlockSpec((tb, tb), lambda b, t: (0, 0)),
                  pl.BlockSpec((256, 512), lambda b, t: (0, 0))],
        out_specs=[tok(512), st_spec],
        out_shape=[jax.ShapeDtypeStruct((bsz, seq, 512), F32),
                   jax.ShapeDtypeStruct((bsz, H_G, DK_G, DV_G), F32)],
        scratch_shapes=[pltpu.VMEM((H_G * DV_G, H_G * DK_G), F32),
                        pltpu.VMEM((H_G * DK_G, H_G * DV_G), F32), pltpu.VMEM((tb, 256), F32)],
        compiler_params=_cparams(("parallel", "arbitrary")),
        name="gla",
    )(q, k, v, lf, s0, jnp.asarray(tri), jnp.asarray(g2, BF16))


def _t5_shifted_bias(n, tab_ref, h):
    nf = jnp.maximum(n, 1).astype(F32)
    large = T5_MAX_EXACT + (jnp.log(nf / T5_MAX_EXACT) / math.log(T5_MAX_DIST / T5_MAX_EXACT)
                            * (N_BUCKETS - T5_MAX_EXACT)).astype(jnp.int32)
    large = jnp.minimum(large, N_BUCKETS - 1)
    bucket = jnp.where(n < T5_MAX_EXACT, n, large)
    far = tab_ref[h * N_BUCKETS + N_BUCKETS - 1]
    val = jnp.zeros(n.shape, F32)
    for b in range(N_BUCKETS - 1):
        val = jnp.where(bucket == b, (tab_ref[h * N_BUCKETS + b] - far) * LOG2E, val)
    return val


def _bias_prompt_body(tab_ref, o_ref, *, T, RB):
    h = pl.program_id(0)
    t = pl.program_id(1)
    for r0 in range(0, T, RB):
        r = lax.broadcasted_iota(jnp.int32, (RB, T), 0) + r0
        c = lax.broadcasted_iota(jnp.int32, (RB, T), 1)
        n = r - c + t * T
        val = _t5_shifted_bias(jnp.maximum(n, 0), tab_ref, h)
        o_ref[0, 0, r0:r0 + RB, :] = jnp.where(n >= 0, val, NEG)


def _bias_prompt_tiles(tab, *, T):
    assert T >= T5_MAX_DIST
    return pl.pallas_call(
        functools.partial(_bias_prompt_body, T=T, RB=64),
        grid=(H_D, 2),
        in_specs=[pl.BlockSpec(memory_space=pltpu.SMEM)],
        out_specs=pl.BlockSpec((1, 1, T, T), lambda h, t: (h, t, 0, 0)),
        out_shape=jax.ShapeDtypeStruct((H_D, 2, T, T), F32),
        compiler_params=_cparams(("parallel", "parallel")),
        name="bias_prompt",
    )(tab)


def _bias_paged_body(tab_ref, o_ref, *, LQ):
    t = pl.program_id(0)
    rh = 2 * LQ
    r = lax.broadcasted_iota(jnp.int32, (rh, PAGE), 0)
    c = lax.broadcasted_iota(jnp.int32, (rh, PAGE), 1)
    n = (r - _idiv(r, LQ) * LQ) - c + (1 - t) * PAGE
    for h in range(H_D):
        val = _t5_shifted_bias(jnp.maximum(n, 0), tab_ref, h)
        o_ref[0, h * rh:(h + 1) * rh, :] = jnp.where(n >= 0, val, NEG)


def _bias_paged_tiles(tab, *, LQ):
    assert PAGE >= T5_MAX_DIST
    R = H_D * 2 * LQ
    return pl.pallas_call(
        functools.partial(_bias_paged_body, LQ=LQ),
        grid=(2,),
        in_specs=[pl.BlockSpec(memory_space=pltpu.SMEM)],
        out_specs=pl.BlockSpec((1, R, PAGE), lambda t: (t, 0, 0)),
        out_shape=jax.ShapeDtypeStruct((2, R, PAGE), F32),
        compiler_params=_cparams(("parallel",)),
        name="bias_paged",
    )(tab)


def _dattn_body(q_ref, k_ref, v_ref, bias_ref, o_ref, qm_scr, m_scr, l_scr, acc_scr, *, T):
    qi = pl.program_id(2)
    m_scr[...] = jnp.full_like(m_scr, NEG)
    l_scr[...] = jnp.zeros_like(l_scr)
    acc_scr[...] = jnp.zeros_like(acc_scr)
    q = q_ref[0]
    lane = lax.broadcasted_iota(jnp.int32, q.shape, 1)
    qm_scr[0] = jnp.where(lane < DK_D, q, jnp.zeros_like(q))
    qm_scr[1] = jnp.where(lane >= DK_D, q, jnp.zeros_like(q))

    def step(ki, bias):
        r0 = pl.multiple_of(ki * T, T)
        k = k_ref[0, pl.ds(r0, T), :]
        v = v_ref[0, pl.ds(r0, T), :]
        for mp in range(2):
            s = _nt_dot(qm_scr[mp], k)
            if bias is not None:
                s = s + bias
            m_prev = m_scr[mp]
            m_new = jnp.maximum(m_prev, jnp.max(s, axis=-1, keepdims=True))
            corr = jnp.exp2(m_prev - m_new)
            p = jnp.exp2(s - jnp.concatenate([m_new] * (T // LANES), axis=1))
            l_scr[mp] = l_scr[mp] * corr + jnp.sum(p, axis=-1, keepdims=True)
            acc_scr[mp] = acc_scr[mp] * corr + jnp.dot(p.astype(BF16), v, preferred_element_type=F32)
            m_scr[mp] = m_new

    def far_step(ki, carry):
        step(ki, None)
        return carry

    lax.fori_loop(0, jnp.maximum(qi - 1, 0), far_step, 0)

    @pl.when(qi >= 1)
    def _():
        step(qi - 1, bias_ref[0, 1])

    step(qi, bias_ref[0, 0])
    for mp in range(2):
        o_ref[0, :, mp * DV_D:(mp + 1) * DV_D] = acc_scr[mp] / l_scr[mp]


def _dattn_prompt(q, k, v, bias_tiles, *, T):
    bsz, seq, _ = q.shape
    kv_spec = pl.BlockSpec((1, seq, 128), lambda b, h, qi: (b, 0, h))
    return pl.pallas_call(
        functools.partial(_dattn_body, T=T),
        grid=(bsz, H_D, seq // T),
        in_specs=[pl.BlockSpec((1, T, 128), lambda b, h, qi: (b, qi, h)), kv_spec, kv_spec,
                  pl.BlockSpec((1, 2, T, T), lambda b, h, qi: (h, 0, 0, 0))],
        out_specs=pl.BlockSpec((1, T, 2 * DV_D), lambda b, h, qi: (b, qi, h)),
        out_shape=jax.ShapeDtypeStruct((bsz, seq, H_D * 2 * DV_D), F32),
        scratch_shapes=[pltpu.VMEM((2, T, 128), BF16), pltpu.VMEM((2, T, LANES), F32),
                        pltpu.VMEM((2, T, LANES), F32), pltpu.VMEM((2, T, DV_D), F32)],
        compiler_params=_cparams(("parallel", "parallel", "arbitrary")),
        name="diff_attn_prompt",
    )(q, k, v, bias_tiles)


def _paged_body(pt_ref, q_ref, kn_ref, vn_ref, bias_ref, *refs, P, LQ):
    k_refs = refs[:P]
    v_refs = refs[P:2 * P]
    o_ref = refs[2 * P]
    wq_scr, m_scr, l_scr, acc_scr = refs[2 * P + 1:]
    j = pl.program_id(1)
    RH = 2 * LQ

    def update(s, v_heads):
        n = s.shape[1] // PAGE
        m_prev = m_scr[...]
        m_new = jnp.maximum(m_prev, jnp.max(s, axis=-1, keepdims=True))
        corr = jnp.exp2(m_prev - m_new)
        p = jnp.exp2(s - jnp.concatenate([m_new] * n, axis=1))
        l_scr[...] = l_scr[...] * corr + jnp.sum(p, axis=-1, keepdims=True)
        pb = p.astype(BF16)
        pv = [jnp.dot(pb[h * RH:(h + 1) * RH, :], v_heads[h], preferred_element_type=F32)
              for h in range(H_D)]
        acc_scr[...] = acc_scr[...] * corr + jnp.concatenate(pv, axis=0)
        m_scr[...] = m_new

    @pl.when(j == 0)
    def _():
        q = q_ref[0]
        qt = jnp.concatenate([q] * (H_D * 2), axis=0)
        rr = lax.broadcasted_iota(jnp.int32, qt.shape, 0)
        cc = lax.broadcasted_iota(jnp.int32, qt.shape, 1)
        wq_scr[...] = jnp.where(_idiv(rr, LQ) == _idiv(cc, DK_D), qt, jnp.zeros_like(qt))
        m_scr[...] = jnp.full_like(m_scr, NEG)
        l_scr[...] = jnp.zeros_like(l_scr)
        acc_scr[...] = jnp.zeros_like(acc_scr)
        pad = jnp.zeros((PAGE - LQ, H_D * 2 * DK_D), BF16)
        kn = jnp.concatenate([kn_ref[0], pad], axis=0)
        vn = jnp.concatenate([vn_ref[0], pad], axis=0)
        update(_nt_dot(wq_scr[...], kn) + bias_ref[1],
               [vn[:, h * DV_D:(h + 1) * DV_D] for h in range(H_D)])

    is_last = j == pl.num_programs(1) - 1
    kt = jnp.concatenate([k_refs[i][0].astype(BF16) for i in range(P)], axis=1)
    s = jnp.dot(wq_scr[...], kt, preferred_element_type=F32)
    last_bias = bias_ref[0] * jnp.where(is_last, 1.0, 0.0)
    s = jnp.concatenate([s[:, :(P - 1) * PAGE], s[:, (P - 1) * PAGE:] + last_bias], axis=1)
    v_heads = [jnp.concatenate([v_refs[i][0, pl.ds(h, PAGE, stride=H_D), :].astype(BF16)
                                for i in range(P)], axis=0) for h in range(H_D)]
    update(s, v_heads)

    @pl.when(is_last)
    def _():
        o_ref[0] = acc_scr[...] / l_scr[...]


def _dattn_paged(q, k_new, v_new, kt_pool, v_pool, page_table, bias_tiles, *, P):
    bd, lq, _ = q.shape
    n_pages = page_table.shape[1]
    R = H_D * 2 * lq
    tok = pl.BlockSpec((1, lq, 512), lambda b, j, pt: (b, 0, 0))

    def page_spec(i):
        return pl.BlockSpec((1, 512, PAGE), lambda b, j, pt: (pt[b * n_pages + j * P + i], 0, 0))

    grid_spec = pltpu.PrefetchScalarGridSpec(
        num_scalar_prefetch=1,
        grid=(bd, n_pages // P),
        in_specs=[tok, tok, tok, pl.BlockSpec((2, R, PAGE), lambda b, j, pt: (0, 0, 0))]
        + [page_spec(i) for i in range(P)] * 2,
        out_specs=pl.BlockSpec((1, R, DV_D), lambda b, j, pt: (b, 0, 0)),
        scratch_shapes=[pltpu.VMEM((R, 512), BF16), pltpu.VMEM((R, LANES), F32),
                        pltpu.VMEM((R, LANES), F32), pltpu.VMEM((R, DV_D), F32)],
    )
    return pl.pallas_call(
        functools.partial(_paged_body, P=P, LQ=lq),
        grid_spec=grid_spec,
        out_shape=jax.ShapeDtypeStruct((bd, R, DV_D), F32),
        compiler_params=_cparams(("parallel", "arbitrary")),
        name="diff_attn_paged",
    )(page_table.reshape(-1), q, k_new, v_new, bias_tiles, *([kt_pool] * P), *([v_pool] * P))


def _mixout_body(x_ref, og_ref, rg_ref, od_ref, lam4_ref, gn_ref, sn_ref, wo_ref, o_ref, *, lam_init):
    lq1 = lam4_ref[0:1, :]
    lk1 = lam4_ref[1:2, :]
    lq2 = lam4_ref[2:3, :]
    lk2 = lam4_ref[3:4, :]
    lam = (jnp.exp(jnp.sum(lq1 * lk1, axis=-1, keepdims=True))
           - jnp.exp(jnp.sum(lq2 * lk2, axis=-1, keepdims=True)) + lam_init)
    og = og_ref[...]
    rg = rg_ref[...]
    od = od_ref[...]
    gn = gn_ref[...]
    sn = sn_ref[...]
    parts = []
    for h in range(H_G):
        t = og[:, h * DV_G:(h + 1) * DV_G]
        r = rg[:, h * DV_G:(h + 1) * DV_G]
        parts.append(_rms_full(t, gn) * (r * jax.nn.sigmoid(r)))
    for h in range(H_D):
        d = od[:, h * 2 * DV_D:h * 2 * DV_D + DV_D] - lam * od[:, h * 2 * DV_D + DV_D:(h + 1) * 2 * DV_D]
        parts.append(_rms_full(d, sn) * (1.0 - lam_init))
    cat = jnp.concatenate(parts, axis=-1).astype(BF16)
    o_ref[...] = x_ref[...] + jnp.dot(cat, wo_ref[...], preferred_element_type=F32)


def _mixout(x, og, rg, od, lam4, gn, sn, wo, *, tm, lam_init):
    n, d = x.shape
    full = lambda shape: pl.BlockSpec(shape, lambda i: (0, 0))
    row = lambda w: pl.BlockSpec((tm, w), lambda i: (i, 0))
    return pl.pallas_call(
        functools.partial(_mixout_body, lam_init=lam_init),
        grid=(n // tm,),
        in_specs=[row(d), row(512), row(512), row(1024), full((4, DK_D)), full((1, DV_G)),
                  full((1, DV_D)), full(wo.shape)],
        out_specs=row(d),
        out_shape=jax.ShapeDtypeStruct((n, d), F32),
        compiler_params=_cparams(("parallel",)),
        name="mixer_out",
    )(x, og, rg, od, lam4, gn, sn, wo)


def _proj_body(x_ref, g_ref, w_ref, hn_ref, *o_refs, head_norm, scale, dh):
    h = _rms_full(x_ref[...], g_ref[...]).astype(BF16)
    y = jnp.dot(h, w_ref[...], preferred_element_type=F32)
    if head_norm:
        hn = hn_ref[...]
        y = jnp.concatenate(
            [_rms_full(y[:, i * dh:(i + 1) * dh], hn) for i in range(y.shape[1] // dh)], axis=-1)
    y = y * scale
    for o_ref in o_refs:
        o_ref[...] = y.astype(o_ref.dtype)


def _proj(x, g, w, hn, *, tm, head_norm, scale, out_dtypes):
    n, d = x.shape
    dout = w.shape[1]
    dh = hn.shape[1]
    full = lambda shape: pl.BlockSpec(shape, lambda i: (0, 0))
    row = lambda wd: pl.BlockSpec((tm, wd), lambda i: (i, 0))
    return pl.pallas_call(
        functools.partial(_proj_body, head_norm=head_norm, scale=scale, dh=dh),
        grid=(n // tm,),
        in_specs=[row(d), full((1, d)), full(w.shape), full((1, dh))],
        out_specs=[row(dout) for _ in out_dtypes],
        out_shape=[jax.ShapeDtypeStruct((n, dout), dt) for dt in out_dtypes],
        compiler_params=_cparams(("parallel",)),
        name="proj",
    )(x, g, w, hn)


def _xattn_body(q_ref, k_ref, v_ref, o_ref, *, dh):
    q = q_ref[0]
    k = k_ref[0].astype(BF16)
    v = v_ref[0].astype(BF16)
    outs = []
    for h in range(H_X):
        sl = slice(h * dh, (h + 1) * dh)
        s = _nt_dot(q[:, sl], k[:, sl])
        m = jnp.max(s, axis=-1, keepdims=True)
        p = jnp.exp(s - m)
        p = p / jnp.sum(p, axis=-1, keepdims=True)
        outs.append(jnp.dot(p.astype(BF16), v[:, sl], preferred_element_type=F32))
    o_ref[0] = jnp.concatenate(outs, axis=-1).astype(o_ref.dtype)


def _xattn(q, k, v, *, tq):
    bsz, seq, d = q.shape
    m = k.shape[1]
    tq = min(tq, seq)
    kv = pl.BlockSpec((1, m, d), lambda b, i: (b, 0, 0))
    return pl.pallas_call(
        functools.partial(_xattn_body, dh=d // H_X),
        grid=(bsz, seq // tq),
        in_specs=[pl.BlockSpec((1, tq, d), lambda b, i: (b, i, 0)), kv, kv],
        out_specs=pl.BlockSpec((1, tq, d), lambda b, i: (b, i, 0)),
        out_shape=jax.ShapeDtypeStruct((bsz, seq, d), BF16),
        compiler_params=_cparams(("parallel", "parallel")),
        name="xattn",
    )(q, k, v)


def _xattn_cached_body(q_ref, k_ref, v_ref, o_ref, *, dh, M):
    q = q_ref[0]
    nlb = dh // LANES
    stride = nlb * H_X
    lq = q.shape[0]
    s_heads = []
    for h in range(H_X):
        s = None
        for lb in range(nlb):
            kh = k_ref[0, pl.ds(lb * H_X + h, M, stride=stride), :].astype(BF16)
            t = _nt_dot(q[:, h * dh + lb * LANES:h * dh + (lb + 1) * LANES], kh)
            s = t if s is None else s + t
        s_heads.append(s)
    s = jnp.concatenate(s_heads, axis=0)
    m = jnp.max(s, axis=-1, keepdims=True)
    p = jnp.exp(s - m)
    p = p / jnp.sum(p, axis=-1, keepdims=True)
    outs = []
    for h in range(H_X):
        ph = p[h * lq:(h + 1) * lq, :].astype(BF16)
        for lb in range(nlb):
            vh = v_ref[0, pl.ds(lb * H_X + h, M, stride=stride), :].astype(BF16)
            outs.append(jnp.dot(ph, vh, preferred_element_type=F32))
    o_ref[0] = jnp.concatenate(outs, axis=-1).astype(o_ref.dtype)


def _xattn_cached(q, k_rows, v_rows, *, M):
    bsz, lq, d = q.shape
    kv = pl.BlockSpec((1, k_rows.shape[1], LANES), lambda b: (b, 0, 0))
    return pl.pallas_call(
        functools.partial(_xattn_cached_body, dh=d // H_X, M=M),
        grid=(bsz,),
        in_specs=[pl.BlockSpec((1, lq, d), lambda b: (b, 0, 0)), kv, kv],
        out_specs=pl.BlockSpec((1, lq, d), lambda b: (b, 0, 0)),
        out_shape=jax.ShapeDtypeStruct((bsz, lq, d), BF16),
        compiler_params=_cparams(("parallel",)),
        name="xattn_cached",
    )(q, k_rows, v_rows)


def _resproj_body(x_ref, a_ref, w_ref, o_ref):
    o_ref[...] = x_ref[...] + jnp.dot(a_ref[...], w_ref[...], preferred_element_type=F32)


def _resproj(x, a, w, *, tm):
    n, d = x.shape
    row = lambda wd: pl.BlockSpec((tm, wd), lambda i: (i, 0))
    return pl.pallas_call(
        _resproj_body,
        grid=(n // tm,),
        in_specs=[row(d), row(a.shape[1]), pl.BlockSpec(w.shape, lambda i: (0, 0))],
        out_specs=row(d),
        out_shape=jax.ShapeDtypeStruct((n, d), F32),
        compiler_params=_cparams(("parallel",)),
        name="resproj",
    )(x, a, w)


def kernel(x_prompt, x_sample, mem_prompt, cache_diff_k, cache_diff_v, state_gla, cache_mem_k, cache_mem_v, page_table, rel_bias, ffn1_norm, ffn1_w_gate, ffn1_w_up, ffn1_w_down, mix_norm, w_in, gla_w_gate_up, gla_b_gate, gla_out_norm, diff_q_norm, diff_k_norm, diff_lam_q1, diff_lam_k1, diff_lam_q2, diff_lam_k2, diff_subln, w_out, xattn_norm, mem_norm, xattn_wq, xattn_wk, xattn_wv, xattn_q_norm, xattn_k_norm, xattn_wo, ffn2_norm, ffn2_w_gate, ffn2_w_up, ffn2_w_down):
    depth = ffn1_norm.shape[0]
    B, L, D = x_prompt.shape
    Bd, Ld, _ = x_sample.shape
    M = mem_prompt.shape[1]
    n_pool = cache_diff_k.shape[1]
    T = 512
    TM = 512
    P_PAGES = 16

    xp = x_prompt.reshape(B * L, D)
    xs = x_sample.reshape(Bd * Ld, D)
    outs = {k: [] for k in ("kp", "vp", "ks", "vs", "sp", "ss", "mk", "mv")}

    g64 = jnp.asarray(np.kron(np.eye(512 // DK_D, dtype=np.float32), np.ones((DK_D, DK_D), np.float32)), BF16)

    for l in range(depth):
        lam_init = 0.8 - 0.6 * math.exp(-0.3 * l)
        row = lambda a: a[l].reshape(1, -1)

        w1g, w1u, w1d = (w[l].astype(BF16) for w in (ffn1_w_gate, ffn1_w_up, ffn1_w_down))
        w2g, w2u, w2d = (w[l].astype(BF16) for w in (ffn2_w_gate, ffn2_w_up, ffn2_w_down))
        wi = w_in[l]
        lr0 = 2 * H_G * DK_G + 2 * H_G * DV_G
        w_main = jnp.concatenate([wi[:, :lr0], wi[:, lr0 + GLA_RANK:]], axis=1).astype(BF16)
        w_lr = jnp.pad(wi[:, lr0:lr0 + GLA_RANK], ((0, 0), (0, 128 - GLA_RANK))).astype(BF16)
        w_up = jnp.pad(gla_w_gate_up[l], ((0, 128 - GLA_RANK), (0, 0))).astype(BF16)
        qn = jnp.tile(diff_q_norm[l], 2 * H_D).reshape(1, -1)
        kn = jnp.tile(diff_k_norm[l], 2 * H_D).reshape(1, -1)
        lam4 = jnp.stack([diff_lam_q1[l], diff_lam_k1[l], diff_lam_q2[l], diff_lam_k2[l]]).astype(F32)
        wo_mix = w_out[l].astype(BF16)
        wq_x, wk_x, wv_x, wo_x = (w[l].astype(BF16) for w in (xattn_wq, xattn_wk, xattn_wv, xattn_wo))

        tab = rel_bias.astype(F32).T.reshape(-1)
        bias_tiles = _bias_prompt_tiles(tab, T=T)
        bias_paged = _bias_paged_tiles(tab, LQ=Ld)

        xp = _ffn(xp, row(ffn1_norm), w1g, w1u, w1d, tm=1024, tf=256)
        xs = _ffn(xs, row(ffn1_norm), w1g, w1u, w1d, tm=1024, tf=256)

        mix_args = (row(mix_norm), w_main, w_lr, w_up, row(gla_b_gate), qn, kn, g64)
        qg, kg, vg, rg, lf, qd, kdt, kdb, vd, vdb = _mixin(xp, *mix_args, tm=TM, seq=L)
        b3 = lambda a: a.reshape(B, L, -1)
        s0 = jnp.zeros((B, H_G, DK_G, DV_G), F32)
        og, sp = _gla(b3(qg), b3(kg), b3(vg), b3(lf), s0, tb=256)
        od = _dattn_prompt(b3(qd), b3(kdb), b3(vdb), bias_tiles, T=T)
        xp = _mixout(xp, og.reshape(B * L, -1), rg, od.reshape(B * L, -1), lam4, row(gla_out_norm),
                     row(diff_subln), wo_mix, tm=TM, lam_init=lam_init)
        outs["kp"].append(kdt.reshape(B, H_D, 2, DK_D, L).transpose(0, 4, 1, 2, 3))
        outs["vp"].append(vd.reshape(B, L, H_D, DV_D))
        outs["sp"].append(sp)

        qg, kg, vg, rg, lf, qd, kd, kdb, vd, vdb = _mixin(xs, *mix_args, tm=TM)
        s3 = lambda a: a.reshape(Bd, Ld, -1)
        og, ss = _gla(s3(qg), s3(kg), s3(vg), s3(lf), state_gla[l], tb=Ld)
        kt_pool = jnp.transpose(cache_diff_k[l], (0, 2, 3, 4, 1)).reshape(n_pool, H_D * 2 * DK_D, PAGE)
        v_pool = cache_diff_v[l].reshape(n_pool, PAGE * H_D, DV_D)
        od = _dattn_paged(s3(qd), s3(kdb), s3(vdb), kt_pool, v_pool, page_table, bias_paged,
                          P=P_PAGES)
        od = od.reshape(Bd, H_D, 2, Ld, DV_D).transpose(0, 3, 1, 2, 4).reshape(Bd * Ld, -1)
        xs = _mixout(xs, og.reshape(Bd * Ld, -1), rg, od, lam4, row(gla_out_norm), row(diff_subln),
                     wo_mix, tm=TM, lam_init=lam_init)
        outs["ks"].append(kd.reshape(Bd, Ld, H_D, 2, DK_D))
        outs["vs"].append(vd.reshape(Bd, Ld, H_D, DV_D))
        outs["ss"].append(ss)

        dh = D // H_X
        nlb = dh // LANES
        mem2 = mem_prompt.reshape(B * M, D)
        (mk,) = _proj(mem2, row(mem_norm), wk_x, row(xattn_k_norm), tm=TM, head_norm=True, scale=1.0,
                      out_dtypes=(F32,))
        ones_dh = jnp.ones((1, dh), F32)
        (mv,) = _proj(mem2, row(mem_norm), wv_x, ones_dh, tm=TM, head_norm=False, scale=1.0,
                      out_dtypes=(F32,))
        (qp,) = _proj(xp, row(xattn_norm), wq_x, row(xattn_q_norm), tm=TM, head_norm=True,
                      scale=dh ** -0.5, out_dtypes=(BF16,))
        (qs,) = _proj(xs, row(xattn_norm), wq_x, row(xattn_q_norm), tm=TM, head_norm=True,
                      scale=dh ** -0.5, out_dtypes=(BF16,))
        op = _xattn(qp.reshape(B, L, D), mk.reshape(B, M, D), mv.reshape(B, M, D), tq=512)
        cache_rows = lambda c: (c[l].reshape(Bd, M, H_X, nlb, LANES).transpose(0, 1, 3, 2, 4)
                                .reshape(Bd, M * nlb * H_X, LANES))
        osm = _xattn_cached(qs.reshape(Bd, Ld, D), cache_rows(cache_mem_k), cache_rows(cache_mem_v), M=M)
        xp = _resproj(xp, op.reshape(B * L, D), wo_x, tm=TM)
        xs = _resproj(xs, osm.reshape(Bd * Ld, D), wo_x, tm=TM)
        outs["mk"].append(mk.reshape(B, M, H_X, dh))
        outs["mv"].append(mv.reshape(B, M, H_X, dh))

        xp = _ffn(xp, row(ffn2_norm), w2g, w2u, w2d, tm=1024, tf=256)
        xs = _ffn(xs, row(ffn2_norm), w2g, w2u, w2d, tm=1024, tf=256)

    st = lambda key: jnp.stack(outs[key])
    return (xp.reshape(B, L, D), xs.reshape(Bd, Ld, D), st("kp"), st("vp"), st("ks"), st("vs"),
            st("sp"), st("ss"), st("mk"), st("mv"))
```

```python
import functools
import math

import numpy as np
import jax
import jax.numpy as jnp
from jax import lax
from jax.experimental import pallas as pl
from jax.experimental.pallas import tpu as pltpu

F32 = jnp.float32
BF16 = jnp.bfloat16

EPS = 1e-6
NEG = -1e30
LOG2E = 1.0 / math.log(2.0)

H_G, DK_G, DV_G = 4, 64, 128
GLA_RANK = 16
GLA_GATE_NORM = 16.0
GLA_CHUNK = 16
H_D, DK_D, DV_D = 4, 64, 128
N_BUCKETS = 32
T5_MAX_EXACT = N_BUCKETS // 2
T5_MAX_DIST = 128
H_X = 4
PAGE = 128
LANES = 128

VMEM_LIMIT = 56 * 1024 * 1024


def _cparams(sem):
    return pltpu.CompilerParams(dimension_semantics=sem, vmem_limit_bytes=VMEM_LIMIT)


def _rms_full(x, g):
    ms = jnp.mean(x * x, axis=-1, keepdims=True)
    return x * lax.rsqrt(ms + EPS) * g


def _idiv(x, d):
    if d & (d - 1) == 0:
        return lax.shift_right_logical(x, int(math.log2(d)))
    return x // d


def _nt_dot(a, b):
    return lax.dot_general(a, b, (((1,), (1,)), ((), ())), preferred_element_type=F32)


def _ffn_body(x_ref, g_ref, wg_ref, wu_ref, wd_ref, o_ref, h_scr, acc_scr, *, tf):
    h_scr[...] = _rms_full(x_ref[...], g_ref[...]).astype(BF16)
    acc_scr[...] = jnp.zeros_like(acc_scr)

    def chunk(j, carry):
        c0 = pl.multiple_of(j * tf, tf)
        h = h_scr[...]
        gate = jnp.dot(h, wg_ref[:, pl.ds(c0, tf)], preferred_element_type=F32)
        up = jnp.dot(h, wu_ref[:, pl.ds(c0, tf)], preferred_element_type=F32)
        a = (gate * jax.nn.sigmoid(gate) * up).astype(BF16)
        acc_scr[...] += jnp.dot(a, wd_ref[pl.ds(c0, tf), :], preferred_element_type=F32)
        return carry

    lax.fori_loop(0, wg_ref.shape[1] // tf, chunk, 0)
    o_ref[...] = x_ref[...] + 0.5 * acc_scr[...]


def _ffn(x, g, wg, wu, wd, *, tm, tf):
    n, d = x.shape
    resident = lambda shape: pl.BlockSpec(shape, lambda i: (0, 0), pipeline_mode=pl.Buffered(1))
    return pl.pallas_call(
        functools.partial(_ffn_body, tf=tf),
        grid=(n // tm,),
        in_specs=[pl.BlockSpec((tm, d), lambda i: (i, 0)), resident((1, d)),
                  resident(wg.shape), resident(wu.shape), resident(wd.shape)],
        out_specs=pl.BlockSpec((tm, d), lambda i: (i, 0)),
        out_shape=jax.ShapeDtypeStruct((n, d), F32),
        scratch_shapes=[pltpu.VMEM((tm, d), BF16), pltpu.VMEM((tm, d), F32)],
        compiler_params=_cparams(("parallel",)),
        name="ffn",
    )(x, g, wg, wu, wd)


def _mixin_body(x_ref, g_ref, w_ref, wlr_ref, wup_ref, bg_ref, qn_ref, kn_ref, g64_ref,
                qg_ref, kg_ref, vg_ref, rg_ref, lf_ref, qd_ref, kd_ref, kdb_ref, vd_ref, vdb_ref,
                *, k_transposed):
    h = _rms_full(x_ref[...], g_ref[...]).astype(BF16)
    y = jnp.dot(h, w_ref[...], preferred_element_type=F32)
    qg_ref[...] = y[:, 0:256] * (DK_G ** -0.5)
    kg_ref[...] = y[:, 256:512]
    vg_ref[...] = y[:, 512:1024]
    rg_ref[...] = y[:, 1024:1536]
    qd = y[:, 1536:2048]
    kd = y[:, 2048:2560]
    vd = y[:, 2560:3072]

    glr = jnp.dot(h, wlr_ref[...], preferred_element_type=F32)
    z = jnp.dot(glr.astype(BF16), wup_ref[...], preferred_element_type=F32) + bg_ref[...]
    lf_ref[...] = (jnp.minimum(z, 0.0) - jnp.log(1.0 + jnp.exp(-jnp.abs(z)))) * (1.0 / GLA_GATE_NORM)

    def group_rms(t, gain):
        msq = jnp.dot((t * t).astype(BF16), g64_ref[...], preferred_element_type=F32) * (1.0 / DK_D)
        return t * lax.rsqrt(msq + EPS) * gain

    qdn = group_rms(qd, qn_ref[...]) * (DK_D ** -0.5 * LOG2E)
    kdn = group_rms(kd, kn_ref[...])
    qd_ref[...] = qdn.astype(BF16)
    if k_transposed:
        kd_ref[0] = kdn.T
    else:
        kd_ref[...] = kdn
    kdb_ref[...] = kdn.astype(BF16)
    for h in range(H_D):
        vd_ref[pl.ds(h, vd.shape[0], stride=H_D), :] = vd[:, h * DV_D:(h + 1) * DV_D]
    vdb_ref[...] = vd.astype(BF16)


def _mixin(x, g, w_main, w_lr, w_up, b_gate, qn, kn, g64, *, tm, seq=None):
    n, d = x.shape
    full = lambda shape: pl.BlockSpec(shape, lambda i: (0, 0))
    row = lambda w: pl.BlockSpec((tm, w), lambda i: (i, 0))
    outs = [(256, F32), (256, F32), (512, F32), (512, F32), (256, F32),
            (512, BF16), (512, F32), (512, BF16), (512, F32), (512, BF16)]
    out_specs = [row(w) for w, _ in outs]
    out_shape = [jax.ShapeDtypeStruct((n, w), dt) for w, dt in outs]
    if seq is not None:
        nt = seq // tm
        out_specs[6] = pl.BlockSpec((1, 512, tm), lambda i: (i // nt, 0, i % nt))
        out_shape[6] = jax.ShapeDtypeStruct((n // seq, 512, seq), F32)
    out_specs[8] = pl.BlockSpec((tm * H_D, DV_D), lambda i: (i, 0))
    out_shape[8] = jax.ShapeDtypeStruct((n * H_D, DV_D), F32)
    return pl.pallas_call(
        functools.partial(_mixin_body, k_transposed=seq is not None),
        grid=(n // tm,),
        in_specs=[row(d), full((1, d)), full(w_main.shape), full(w_lr.shape), full(w_up.shape),
                  full((1, 256)), full((1, 512)), full((1, 512)), full((512, 512))],
        out_specs=out_specs,
        out_shape=out_shape,
        compiler_params=_cparams(("parallel",)),
        name="mixer_in",
    )(x, g, w_main, w_lr, w_up, b_gate, qn, kn, g64)


def _gla_body(q_ref, k_ref, v_ref, lf_ref, s0_ref, tri_ref, g2_ref, o_ref, so_ref,
              st_scr, nat_scr, b_scr, *, C, n_chunks):
    tb = pl.program_id(1)

    @pl.when(tb == 0)
    def _():
        nat_scr[...] = jnp.zeros_like(nat_scr)
        for h in range(H_G):
            nat_scr[h * DK_G:(h + 1) * DK_G, h * DV_G:(h + 1) * DV_G] = s0_ref[0, h]
        st_scr[...] = nat_scr[...].T

    b_scr[...] = jnp.dot(tri_ref[...], lf_ref[0], preferred_element_type=F32,
                         precision=lax.Precision.HIGHEST)

    rows = lax.broadcasted_iota(jnp.int32, (H_G * DV_G, H_G * DK_G), 0)
    cols = lax.broadcasted_iota(jnp.int32, (H_G * DV_G, H_G * DK_G), 1)
    head_mask = _idiv(rows, DV_G) == _idiv(cols, DK_G)
    t_idx = lax.broadcasted_iota(jnp.int32, (C, H_G * DK_G), 0)

    def chunk(c, carry):
        r0 = pl.multiple_of(c * C, C)
        q = q_ref[0, pl.ds(r0, C), :]
        k = k_ref[0, pl.ds(r0, C), :]
        v = v_ref[0, pl.ds(r0, C), :]
        b = b_scr[pl.ds(r0, C), :]
        st = st_scr[...]

        o = _nt_dot((q * jnp.exp(b)).astype(BF16), st.astype(BF16))

        pieces = []
        for s in range(C):
            rel = jnp.where(t_idx >= s, b - b[s:s + 1, :], NEG)
            pieces.append(q * k[s:s + 1, :] * jnp.exp(rel))
        w = jnp.concatenate(pieces, axis=0).astype(BF16)
        a = jnp.dot(w, g2_ref[...], preferred_element_type=F32)
        for s in range(C):
            o = o + a[s * C:(s + 1) * C, :] * v[s:s + 1, :]
        o_ref[0, pl.ds(r0, C), :] = o

        b_last = b[C - 1:C, :]
        kt = (k * jnp.exp(b_last - b)).astype(BF16)
        upd = lax.dot_general(v.astype(BF16), kt, (((0,), (0,)), ((), ())),
                              preferred_element_type=F32)
        st_scr[...] = st * jnp.exp(b_last) + jnp.where(head_mask, upd, 0.0)
        return carry

    lax.fori_loop(0, n_chunks, chunk, 0, unroll=min(2, n_chunks))

    @pl.when(tb == pl.num_programs(1) - 1)
    def _():
        nat = st_scr[...].T
        for h in range(H_G):
            so_ref[0, h] = nat[h * DK_G:(h + 1) * DK_G, h * DV_G:(h + 1) * DV_G]


def _gla(q, k, v, lf, s0, *, tb):
    bsz, seq, _ = q.shape
    C = math.gcd(seq, GLA_CHUNK)
    tb = min(tb, seq)
    n_chunks = tb // C
    tri = np.zeros((tb, tb), np.float32)
    for c in range(n_chunks):
        tri[c * C:(c + 1) * C, c * C:(c + 1) * C] = np.tril(np.ones((C, C), np.float32))
    g2 = np.zeros((H_G * DK_G, H_G * DV_G), np.float32)
    for h in range(H_G):
        g2[h * DK_G:(h + 1) * DK_G, h * DV_G:(h + 1) * DV_G] = 1.0
    tok = lambda w: pl.BlockSpec((1, tb, w), lambda b, t: (b, t, 0))
    st_spec = pl.BlockSpec((1, H_G, DK_G, DV_G), lambda b, t: (b, 0, 0, 0))
    return pl.pallas_call(
        functools.partial(_gla_body, C=C, n_chunks=n_chunks),
        grid=(bsz, seq // tb),
        in_specs=[tok(256), tok(256), tok(512), tok(256), st_spec,
                  pl.BlockSpec((tb, tb), lambda b, t: (0, 0)),
                  pl.BlockSpec((256, 512), lambda b, t: (0, 0))],
        out_specs=[tok(512), st_spec],
        out_shape=[jax.ShapeDtypeStruct((bsz, seq, 512), F32),
                   jax.ShapeDtypeStruct((bsz, H_G, DK_G, DV_G), F32)],
        scratch_shapes=[pltpu.VMEM((H_G * DV_G, H_G * DK_G), F32),
                        pltpu.VMEM((H_G * DK_G, H_G * DV_G), F32), pltpu.VMEM((tb, 256), F32)],
        compiler_params=_cparams(("parallel", "arbitrary")),
        name="gla",
    )(q, k, v, lf, s0, jnp.asarray(tri), jnp.asarray(g2, BF16))


def _t5_shifted_bias(n, tab_ref, h):
    nf = jnp.maximum(n, 1).astype(F32)
    large = T5_MAX_EXACT + (jnp.log(nf / T5_MAX_EXACT) / math.log(T5_MAX_DIST / T5_MAX_EXACT)
                            * (N_BUCKETS - T5_MAX_EXACT)).astype(jnp.int32)
    large = jnp.minimum(large, N_BUCKETS - 1)
    bucket = jnp.where(n < T5_MAX_EXACT, n, large)
    far = tab_ref[h * N_BUCKETS + N_BUCKETS - 1]
    val = jnp.zeros(n.shape, F32)
    for b in range(N_BUCKETS - 1):
        val = jnp.where(bucket == b, (tab_ref[h * N_BUCKETS + b] - far) * LOG2E, val)
    return val


def _bias_prompt_body(tab_ref, o_ref, *, T, RB):
    h = pl.program_id(0)
    t = pl.program_id(1)
    for r0 in range(0, T, RB):
        r = lax.broadcasted_iota(jnp.int32, (RB, T), 0) + r0
        c = lax.broadcasted_iota(jnp.int32, (RB, T), 1)
        n = r - c + t * T
        val = _t5_shifted_bias(jnp.maximum(n, 0), tab_ref, h)
        o_ref[0, 0, r0:r0 + RB, :] = jnp.where(n >= 0, val, NEG)


def _bias_prompt_tiles(tab, *, T):
    assert T >= T5_MAX_DIST
    return pl.pallas_call(
        functools.partial(_bias_prompt_body, T=T, RB=64),
        grid=(H_D, 2),
        in_specs=[pl.BlockSpec(memory_space=pltpu.SMEM)],
        out_specs=pl.BlockSpec((1, 1, T, T), lambda h, t: (h, t, 0, 0)),
        out_shape=jax.ShapeDtypeStruct((H_D, 2, T, T), F32),
        compiler_params=_cparams(("parallel", "parallel")),
        name="bias_prompt",
    )(tab)


def _bias_paged_body(tab_ref, o_ref, *, LQ):
    t = pl.program_id(0)
    rh = 2 * LQ
    r = lax.broadcasted_iota(jnp.int32, (rh, PAGE), 0)
    c = lax.broadcasted_iota(jnp.int32, (rh, PAGE), 1)
    n = (r - _idiv(r, LQ) * LQ) - c + (1 - t) * PAGE
    for h in range(H_D):
        val = _t5_shifted_bias(jnp.maximum(n, 0), tab_ref, h)
        o_ref[0, h * rh:(h + 1) * rh, :] = jnp.where(n >= 0, val, NEG)


def _bias_paged_tiles(tab, *, LQ):
    assert PAGE >= T5_MAX_DIST
    R = H_D * 2 * LQ
    return pl.pallas_call(
        functools.partial(_bias_paged_body, LQ=LQ),
        grid=(2,),
        in_specs=[pl.BlockSpec(memory_space=pltpu.SMEM)],
        out_specs=pl.BlockSpec((1, R, PAGE), lambda t: (t, 0, 0)),
        out_shape=jax.ShapeDtypeStruct((2, R, PAGE), F32),
        compiler_params=_cparams(("parallel",)),
        name="bias_paged",
    )(tab)


def _dattn_body(q_ref, k_ref, v_ref, bias_ref, o_ref, qm_scr, m_scr, l_scr, acc_scr, *, T):
    qi = pl.program_id(2)
    m_scr[...] = jnp.full_like(m_scr, NEG)
    l_scr[...] = jnp.zeros_like(l_scr)
    acc_scr[...] = jnp.zeros_like(acc_scr)
    q = q_ref[0]
    lane = lax.broadcasted_iota(jnp.int32, q.shape, 1)
    qm_scr[0] = jnp.where(lane < DK_D, q, jnp.zeros_like(q))
    qm_scr[1] = jnp.where(lane >= DK_D, q, jnp.zeros_like(q))

    def step(kj, r0, nr, bias):
        c0 = pl.multiple_of(kj * T, T)
        k = k_ref[0, pl.ds(c0, T), :]
        v = v_ref[0, pl.ds(c0, T), :]
        for mp in range(2):
            s = _nt_dot(qm_scr[mp, r0:r0 + nr, :], k)
            if bias is not None:
                s = s + bias
            m_prev = m_scr[mp, r0:r0 + nr, :]
            m_new = jnp.maximum(m_prev, jnp.max(s, axis=-1, keepdims=True))
            corr = jnp.exp2(m_prev - m_new)
            p = jnp.exp2(s - jnp.concatenate([m_new] * (T // LANES), axis=1))
            l_scr[mp, r0:r0 + nr, :] = l_scr[mp, r0:r0 + nr, :] * corr + jnp.sum(p, axis=-1, keepdims=True)
            acc_scr[mp, r0:r0 + nr, :] = (acc_scr[mp, r0:r0 + nr, :] * corr
                                          + jnp.dot(p.astype(BF16), v, preferred_element_type=F32))
            m_scr[mp, r0:r0 + nr, :] = m_new

    def far_step(kj, carry):
        step(kj, 0, 2 * T, None)
        return carry

    lax.fori_loop(0, jnp.maximum(2 * qi - 1, 0), far_step, 0)

    diag = bias_ref[0, 0]
    sub = bias_ref[0, 1]

    @pl.when(qi >= 1)
    def _():
        step(2 * qi - 1, 0, 2 * T, jnp.concatenate([sub, jnp.zeros_like(sub)], axis=0))

    step(2 * qi, 0, 2 * T, jnp.concatenate([diag, sub], axis=0))
    step(2 * qi + 1, T, T, diag)
    for mp in range(2):
        o_ref[0, :, mp * DV_D:(mp + 1) * DV_D] = acc_scr[mp] / l_scr[mp]


def _dattn_prompt(q, k, v, bias_tiles, *, T):
    bsz, seq, _ = q.shape
    tq = 2 * T
    kv_spec = pl.BlockSpec((1, seq, 128), lambda b, h, qi: (b, 0, h))
    return pl.pallas_call(
        functools.partial(_dattn_body, T=T),
        grid=(bsz, H_D, seq // tq),
        in_specs=[pl.BlockSpec((1, tq, 128), lambda b, h, qi: (b, qi, h)), kv_spec, kv_spec,
                  pl.BlockSpec((1, 2, T, T), lambda b, h, qi: (h, 0, 0, 0))],
        out_specs=pl.BlockSpec((1, tq, 2 * DV_D), lambda b, h, qi: (b, qi, h)),
        out_shape=jax.ShapeDtypeStruct((bsz, seq, H_D * 2 * DV_D), F32),
        scratch_shapes=[pltpu.VMEM((2, tq, 128), BF16), pltpu.VMEM((2, tq, LANES), F32),
                        pltpu.VMEM((2, tq, LANES), F32), pltpu.VMEM((2, tq, DV_D), F32)],
        compiler_params=_cparams(("parallel", "parallel", "arbitrary")),
        name="diff_attn_prompt",
    )(q, k, v, bias_tiles)


def _paged_body(pt_ref, q_ref, kn_ref, vn_ref, bias_ref, *refs, P, LQ):
    k_refs = refs[:P]
    v_refs = refs[P:2 * P]
    o_ref = refs[2 * P]
    wq_scr, m_scr, l_scr, acc_scr = refs[2 * P + 1:]
    j = pl.program_id(1)
    RH = 2 * LQ

    def update(s, v_heads):
        n = s.shape[1] // PAGE
        m_prev = m_scr[...]
        m_new = jnp.maximum(m_prev, jnp.max(s, axis=-1, keepdims=True))
        corr = jnp.exp2(m_prev - m_new)
        p = jnp.exp2(s - jnp.concatenate([m_new] * n, axis=1))
        l_scr[...] = l_scr[...] * corr + jnp.sum(p, axis=-1, keepdims=True)
        pb = p.astype(BF16)
        pv = [jnp.dot(pb[h * RH:(h + 1) * RH, :], v_heads[h], preferred_element_type=F32)
              for h in range(H_D)]
        acc_scr[...] = acc_scr[...] * corr + jnp.concatenate(pv, axis=0)
        m_scr[...] = m_new

    @pl.when(j == 0)
    def _():
        q = q_ref[0]
        qt = jnp.concatenate([q] * (H_D * 2), axis=0)
        rr = lax.broadcasted_iota(jnp.int32, qt.shape, 0)
        cc = lax.broadcasted_iota(jnp.int32, qt.shape, 1)
        wq_scr[...] = jnp.where(_idiv(rr, LQ) == _idiv(cc, DK_D), qt, jnp.zeros_like(qt))
        m_scr[...] = jnp.full_like(m_scr, NEG)
        l_scr[...] = jnp.zeros_like(l_scr)
        acc_scr[...] = jnp.zeros_like(acc_scr)
        pad = jnp.zeros((PAGE - LQ, H_D * 2 * DK_D), BF16)
        kn = jnp.concatenate([kn_ref[0], pad], axis=0)
        vn = jnp.concatenate([vn_ref[0], pad], axis=0)
        update(_nt_dot(wq_scr[...], kn) + bias_ref[1],
               [vn[:, h * DV_D:(h + 1) * DV_D] for h in range(H_D)])

    is_last = j == pl.num_programs(1) - 1
    kt = jnp.concatenate([k_refs[i][0].astype(BF16) for i in range(P)], axis=1)
    s = jnp.dot(wq_scr[...], kt, preferred_element_type=F32)
    last_bias = bias_ref[0] * jnp.where(is_last, 1.0, 0.0)
    s = jnp.concatenate([s[:, :(P - 1) * PAGE], s[:, (P - 1) * PAGE:] + last_bias], axis=1)
    v_heads = [jnp.concatenate([v_refs[i][0, pl.ds(h, PAGE, stride=H_D), :].astype(BF16)
                                for i in range(P)], axis=0) for h in range(H_D)]
    update(s, v_heads)

    @pl.when(is_last)
    def _():
        o_ref[0] = acc_scr[...] / l_scr[...]


def _dattn_paged(q, k_new, v_new, kt_pool, v_pool, page_table, bias_tiles, *, P):
    bd, lq, _ = q.shape
    n_pages = page_table.shape[1]
    R = H_D * 2 * lq
    tok = pl.BlockSpec((1, lq, 512), lambda b, j, pt: (b, 0, 0))

    def page_spec(i):
        return pl.BlockSpec((1, 512, PAGE), lambda b, j, pt: (pt[b * n_pages + j * P + i], 0, 0))

    grid_spec = pltpu.PrefetchScalarGridSpec(
        num_scalar_prefetch=1,
        grid=(bd, n_pages // P),
        in_specs=[tok, tok, tok, pl.BlockSpec((2, R, PAGE), lambda b, j, pt: (0, 0, 0))]
        + [page_spec(i) for i in range(P)] * 2,
        out_specs=pl.BlockSpec((1, R, DV_D), lambda b, j, pt: (b, 0, 0)),
        scratch_shapes=[pltpu.VMEM((R, 512), BF16), pltpu.VMEM((R, LANES), F32),
                        pltpu.VMEM((R, LANES), F32), pltpu.VMEM((R, DV_D), F32)],
    )
    return pl.pallas_call(
        functools.partial(_paged_body, P=P, LQ=lq),
        grid_spec=grid_spec,
        out_shape=jax.ShapeDtypeStruct((bd, R, DV_D), F32),
        compiler_params=_cparams(("parallel", "arbitrary")),
        name="diff_attn_paged",
    )(page_table.reshape(-1), q, k_new, v_new, bias_tiles, *([kt_pool] * P), *([v_pool] * P))


def _mixout_math(x, og, rg, od, lam4, gn, sn, wo, lam_init):
    lam = (jnp.exp(jnp.sum(lam4[0:1, :] * lam4[1:2, :], axis=-1, keepdims=True))
           - jnp.exp(jnp.sum(lam4[2:3, :] * lam4[3:4, :], axis=-1, keepdims=True)) + lam_init)
    parts = []
    for h in range(H_G):
        t = og[:, h * DV_G:(h + 1) * DV_G]
        r = rg[:, h * DV_G:(h + 1) * DV_G]
        parts.append(_rms_full(t, gn) * (r * jax.nn.sigmoid(r)))
    for h in range(H_D):
        d = od[:, h * 2 * DV_D:h * 2 * DV_D + DV_D] - lam * od[:, h * 2 * DV_D + DV_D:(h + 1) * 2 * DV_D]
        parts.append(_rms_full(d, sn) * (1.0 - lam_init))
    cat = jnp.concatenate(parts, axis=-1).astype(BF16)
    return x + jnp.dot(cat, wo, preferred_element_type=F32)


def _mixout_body(x_ref, og_ref, rg_ref, od_ref, lam4_ref, gn_ref, sn_ref, wo_ref, o_ref, *, lam_init):
    o_ref[...] = _mixout_math(x_ref[...], og_ref[...], rg_ref[...], od_ref[...], lam4_ref[...],
                              gn_ref[...], sn_ref[...], wo_ref[...], lam_init)


def _mix_xattn_body(x_ref, og_ref, rg_ref, od_ref, lam4_ref, gn_ref, sn_ref, wo_ref,
                    xg_ref, wq_ref, qn_ref, mk_ref, mv_ref, wox_ref, o_ref, *, lam_init, dh):
    x = _mixout_math(x_ref[...], og_ref[...], rg_ref[...], od_ref[...], lam4_ref[...],
                     gn_ref[...], sn_ref[...], wo_ref[...], lam_init)
    h = _rms_full(x, xg_ref[...]).astype(BF16)
    y = jnp.dot(h, wq_ref[...], preferred_element_type=F32)
    qn = qn_ref[...]
    k = mk_ref[0]
    v = mv_ref[0]
    outs = []
    for i in range(H_X):
        sl = slice(i * dh, (i + 1) * dh)
        q = (_rms_full(y[:, sl], qn) * (dh ** -0.5)).astype(BF16)
        s = _nt_dot(q, k[:, sl])
        p = jnp.exp(s - jnp.max(s, axis=-1, keepdims=True))
        p = p / jnp.sum(p, axis=-1, keepdims=True)
        outs.append(jnp.dot(p.astype(BF16), v[:, sl], preferred_element_type=F32))
    att = jnp.concatenate(outs, axis=-1).astype(BF16)
    o_ref[...] = x + jnp.dot(att, wox_ref[...], preferred_element_type=F32)


def _mix_xattn(x, og, rg, od, lam4, gn, sn, wo, xg, wq, qn, mk, mv, wox, *, tm, seq, lam_init):
    n, d = x.shape
    nt = seq // tm
    resident = lambda shape: pl.BlockSpec(shape, lambda i: (0,) * len(shape), pipeline_mode=pl.Buffered(1))
    row = lambda w: pl.BlockSpec((tm, w), lambda i: (i, 0))
    mem = pl.BlockSpec((1,) + mk.shape[1:], lambda i: (i // nt, 0, 0))
    return pl.pallas_call(
        functools.partial(_mix_xattn_body, lam_init=lam_init, dh=d // H_X),
        grid=(n // tm,),
        in_specs=[row(d), row(512), row(512), row(1024), resident((4, DK_D)), resident((1, DV_G)),
                  resident((1, DV_D)), resident(wo.shape), resident((1, d)), resident(wq.shape),
                  resident(qn.shape), mem, mem, resident(wox.shape)],
        out_specs=row(d),
        out_shape=jax.ShapeDtypeStruct((n, d), F32),
        compiler_params=_cparams(("parallel",)),
        name="mixer_out_xattn",
    )(x, og, rg, od, lam4, gn, sn, wo, xg, wq, qn, mk, mv, wox)


def _mixout(x, og, rg, od, lam4, gn, sn, wo, *, tm, lam_init):
    n, d = x.shape
    full = lambda shape: pl.BlockSpec(shape, lambda i: (0, 0))
    row = lambda w: pl.BlockSpec((tm, w), lambda i: (i, 0))
    return pl.pallas_call(
        functools.partial(_mixout_body, lam_init=lam_init),
        grid=(n // tm,),
        in_specs=[row(d), row(512), row(512), row(1024), full((4, DK_D)), full((1, DV_G)),
                  full((1, DV_D)), full(wo.shape)],
        out_specs=row(d),
        out_shape=jax.ShapeDtypeStruct((n, d), F32),
        compiler_params=_cparams(("parallel",)),
        name="mixer_out",
    )(x, og, rg, od, lam4, gn, sn, wo)


def _proj_body(x_ref, g_ref, w_ref, hn_ref, *o_refs, head_norm, scale, dh):
    h = _rms_full(x_ref[...], g_ref[...]).astype(BF16)
    y = jnp.dot(h, w_ref[...], preferred_element_type=F32)
    if head_norm:
        hn = hn_ref[...]
        y = jnp.concatenate(
            [_rms_full(y[:, i * dh:(i + 1) * dh], hn) for i in range(y.shape[1] // dh)], axis=-1)
    y = y * scale
    for o_ref in o_refs:
        o_ref[...] = y.astype(o_ref.dtype)


def _proj(x, g, w, hn, *, tm, head_norm, scale, out_dtypes):
    n, d = x.shape
    dout = w.shape[1]
    dh = hn.shape[1]
    full = lambda shape: pl.BlockSpec(shape, lambda i: (0, 0))
    row = lambda wd: pl.BlockSpec((tm, wd), lambda i: (i, 0))
    return pl.pallas_call(
        functools.partial(_proj_body, head_norm=head_norm, scale=scale, dh=dh),
        grid=(n // tm,),
        in_specs=[row(d), full((1, d)), full(w.shape), full((1, dh))],
        out_specs=[row(dout) for _ in out_dtypes],
        out_shape=[jax.ShapeDtypeStruct((n, dout), dt) for dt in out_dtypes],
        compiler_params=_cparams(("parallel",)),
        name="proj",
    )(x, g, w, hn)


def _xattn_cached_body(q_ref, k_ref, v_ref, o_ref, *, dh, M):
    q = q_ref[0]
    nlb = dh // LANES
    stride = nlb * H_X
    lq = q.shape[0]
    s_heads = []
    for h in range(H_X):
        s = None
        for lb in range(nlb):
            kh = k_ref[0, pl.ds(lb * H_X + h, M, stride=stride), :].astype(BF16)
            t = _nt_dot(q[:, h * dh + lb * LANES:h * dh + (lb + 1) * LANES], kh)
            s = t if s is None else s + t
        s_heads.append(s)
    s = jnp.concatenate(s_heads, axis=0)
    m = jnp.max(s, axis=-1, keepdims=True)
    p = jnp.exp(s - m)
    p = p / jnp.sum(p, axis=-1, keepdims=True)
    outs = []
    for h in range(H_X):
        ph = p[h * lq:(h + 1) * lq, :].astype(BF16)
        for lb in range(nlb):
            vh = v_ref[0, pl.ds(lb * H_X + h, M, stride=stride), :].astype(BF16)
            outs.append(jnp.dot(ph, vh, preferred_element_type=F32))
    o_ref[0] = jnp.concatenate(outs, axis=-1).astype(o_ref.dtype)


def _xattn_cached(q, k_rows, v_rows, *, M):
    bsz, lq, d = q.shape
    kv = pl.BlockSpec((1, k_rows.shape[1], LANES), lambda b: (b, 0, 0))
    return pl.pallas_call(
        functools.partial(_xattn_cached_body, dh=d // H_X, M=M),
        grid=(bsz,),
        in_specs=[pl.BlockSpec((1, lq, d), lambda b: (b, 0, 0)), kv, kv],
        out_specs=pl.BlockSpec((1, lq, d), lambda b: (b, 0, 0)),
        out_shape=jax.ShapeDtypeStruct((bsz, lq, d), BF16),
        compiler_params=_cparams(("parallel",)),
        name="xattn_cached",
    )(q, k_rows, v_rows)


def _resproj_body(x_ref, a_ref, w_ref, o_ref):
    o_ref[...] = x_ref[...] + jnp.dot(a_ref[...], w_ref[...], preferred_element_type=F32)


def _resproj(x, a, w, *, tm):
    n, d = x.shape
    row = lambda wd: pl.BlockSpec((tm, wd), lambda i: (i, 0))
    return pl.pallas_call(
        _resproj_body,
        grid=(n // tm,),
        in_specs=[row(d), row(a.shape[1]), pl.BlockSpec(w.shape, lambda i: (0, 0))],
        out_specs=row(d),
        out_shape=jax.ShapeDtypeStruct((n, d), F32),
        compiler_params=_cparams(("parallel",)),
        name="resproj",
    )(x, a, w)


def kernel(x_prompt, x_sample, mem_prompt, cache_diff_k, cache_diff_v, state_gla, cache_mem_k, cache_mem_v, page_table, rel_bias, ffn1_norm, ffn1_w_gate, ffn1_w_up, ffn1_w_down, mix_norm, w_in, gla_w_gate_up, gla_b_gate, gla_out_norm, diff_q_norm, diff_k_norm, diff_lam_q1, diff_lam_k1, diff_lam_q2, diff_lam_k2, diff_subln, w_out, xattn_norm, mem_norm, xattn_wq, xattn_wk, xattn_wv, xattn_q_norm, xattn_k_norm, xattn_wo, ffn2_norm, ffn2_w_gate, ffn2_w_up, ffn2_w_down):
    depth = ffn1_norm.shape[0]
    B, L, D = x_prompt.shape
    Bd, Ld, _ = x_sample.shape
    M = mem_prompt.shape[1]
    n_pool = cache_diff_k.shape[1]
    T = 512
    TM = 512
    P_PAGES = 32

    xp = x_prompt.reshape(B * L, D)
    xs = x_sample.reshape(Bd * Ld, D)
    outs = {k: [] for k in ("kp", "vp", "ks", "vs", "sp", "ss", "mk", "mv")}

    g64 = jnp.asarray(np.kron(np.eye(512 // DK_D, dtype=np.float32), np.ones((DK_D, DK_D), np.float32)), BF16)

    for l in range(depth):
        lam_init = 0.8 - 0.6 * math.exp(-0.3 * l)
        row = lambda a: a[l].reshape(1, -1)

        w1g, w1u, w1d = (w[l].astype(BF16) for w in (ffn1_w_gate, ffn1_w_up, ffn1_w_down))
        w2g, w2u, w2d = (w[l].astype(BF16) for w in (ffn2_w_gate, ffn2_w_up, ffn2_w_down))
        wi = w_in[l]
        lr0 = 2 * H_G * DK_G + 2 * H_G * DV_G
        w_main = jnp.concatenate([wi[:, :lr0], wi[:, lr0 + GLA_RANK:]], axis=1).astype(BF16)
        w_lr = jnp.pad(wi[:, lr0:lr0 + GLA_RANK], ((0, 0), (0, 128 - GLA_RANK))).astype(BF16)
        w_up = jnp.pad(gla_w_gate_up[l], ((0, 128 - GLA_RANK), (0, 0))).astype(BF16)
        qn = jnp.tile(diff_q_norm[l], 2 * H_D).reshape(1, -1)
        kn = jnp.tile(diff_k_norm[l], 2 * H_D).reshape(1, -1)
        lam4 = jnp.stack([diff_lam_q1[l], diff_lam_k1[l], diff_lam_q2[l], diff_lam_k2[l]]).astype(F32)
        wo_mix = w_out[l].astype(BF16)
        wq_x, wk_x, wv_x, wo_x = (w[l].astype(BF16) for w in (xattn_wq, xattn_wk, xattn_wv, xattn_wo))

        tab = rel_bias.astype(F32).T.reshape(-1)
        bias_tiles = _bias_prompt_tiles(tab, T=T)
        bias_paged = _bias_paged_tiles(tab, LQ=Ld)

        xp = _ffn(xp, row(ffn1_norm), w1g, w1u, w1d, tm=1024, tf=256)
        xs = _ffn(xs, row(ffn1_norm), w1g, w1u, w1d, tm=1024, tf=256)

        mix_args = (row(mix_norm), w_main, w_lr, w_up, row(gla_b_gate), qn, kn, g64)
        qg, kg, vg, rg, lf, qd, kdt, kdb, vd, vdb = _mixin(xp, *mix_args, tm=TM, seq=L)
        b3 = lambda a: a.reshape(B, L, -1)
        s0 = jnp.zeros((B, H_G, DK_G, DV_G), F32)
        og, sp = _gla(b3(qg), b3(kg), b3(vg), b3(lf), s0, tb=256)
        od = _dattn_prompt(b3(qd), b3(kdb), b3(vdb), bias_tiles, T=T)
        dh = D // H_X
        mem2 = mem_prompt.reshape(B * M, D)
        mk, mkb = _proj(mem2, row(mem_norm), wk_x, row(xattn_k_norm), tm=TM, head_norm=True, scale=1.0,
                        out_dtypes=(F32, BF16))
        mv, mvb = _proj(mem2, row(mem_norm), wv_x, jnp.ones((1, dh), F32), tm=TM, head_norm=False,
                        scale=1.0, out_dtypes=(F32, BF16))
        xp = _mix_xattn(xp, og.reshape(B * L, -1), rg, od.reshape(B * L, -1), lam4, row(gla_out_norm),
                        row(diff_subln), wo_mix, row(xattn_norm), wq_x, row(xattn_q_norm),
                        mkb.reshape(B, M, D), mvb.reshape(B, M, D), wo_x, tm=TM, seq=L, lam_init=lam_init)
        outs["mk"].append(mk.reshape(B, M, H_X, dh))
        outs["mv"].append(mv.reshape(B, M, H_X, dh))
        outs["kp"].append(kdt.reshape(B, H_D, 2, DK_D, L).transpose(0, 4, 1, 2, 3))
        outs["vp"].append(vd.reshape(B, L, H_D, DV_D))
        outs["sp"].append(sp)

        qg, kg, vg, rg, lf, qd, kd, kdb, vd, vdb = _mixin(xs, *mix_args, tm=TM)
        s3 = lambda a: a.reshape(Bd, Ld, -1)
        og, ss = _gla(s3(qg), s3(kg), s3(vg), s3(lf), state_gla[l], tb=Ld)
        kt_pool = jnp.transpose(cache_diff_k[l], (0, 2, 3, 4, 1)).reshape(n_pool, H_D * 2 * DK_D, PAGE)
        v_pool = cache_diff_v[l].reshape(n_pool, PAGE * H_D, DV_D)
        od = _dattn_paged(s3(qd), s3(kdb), s3(vdb), kt_pool, v_pool, page_table, bias_paged,
                          P=P_PAGES)
        od = od.reshape(Bd, H_D, 2, Ld, DV_D).transpose(0, 3, 1, 2, 4).reshape(Bd * Ld, -1)
        xs = _mixout(xs, og.reshape(Bd * Ld, -1), rg, od, lam4, row(gla_out_norm), row(diff_subln),
                     wo_mix, tm=TM, lam_init=lam_init)
        outs["ks"].append(kd.reshape(Bd, Ld, H_D, 2, DK_D))
        outs["vs"].append(vd.reshape(Bd, Ld, H_D, DV_D))
        outs["ss"].append(ss)

        nlb = dh // LANES
        (qs,) = _proj(xs, row(xattn_norm), wq_x, row(xattn_q_norm), tm=TM, head_norm=True,
                      scale=dh ** -0.5, out_dtypes=(BF16,))
        cache_rows = lambda c: (c[l].reshape(Bd, M, H_X, nlb, LANES).transpose(0, 1, 3, 2, 4)
                                .reshape(Bd, M * nlb * H_X, LANES))
        osm = _xattn_cached(qs.reshape(Bd, Ld, D), cache_rows(cache_mem_k), cache_rows(cache_mem_v), M=M)
        xs = _resproj(xs, osm.reshape(Bd * Ld, D), wo_x, tm=TM)

        xp = _ffn(xp, row(ffn2_norm), w2g, w2u, w2d, tm=1024, tf=256)
        xs = _ffn(xs, row(ffn2_norm), w2g, w2u, w2d, tm=1024, tf=256)

    st = lambda key: jnp.stack(outs[key])
    return (xp.reshape(B, L, D), xs.reshape(Bd, Ld, D), st("kp"), st("vp"), st("ks"), st("vs"),
            st("sp"), st("ss"), st("mk"), st("mv"))
```

```python
import functools
import math

import numpy as np
import jax
import jax.numpy as jnp
from jax import lax
from jax.experimental import pallas as pl
from jax.experimental.pallas import tpu as pltpu

F32 = jnp.float32
BF16 = jnp.bfloat16

EPS = 1e-6
NEG = -1e30
LOG2E = 1.0 / math.log(2.0)

H_G, DK_G, DV_G = 4, 64, 128
GLA_RANK = 16
GLA_GATE_NORM = 16.0
GLA_CHUNK = 16
H_D, DK_D, DV_D = 4, 64, 128
N_BUCKETS = 32
T5_MAX_EXACT = N_BUCKETS // 2
T5_MAX_DIST = 128
H_X = 4
PAGE = 128
LANES = 128

VMEM_LIMIT = 56 * 1024 * 1024


def _cparams(sem):
    return pltpu.CompilerParams(dimension_semantics=sem, vmem_limit_bytes=VMEM_LIMIT)


def _rms_full(x, g):
    ms = jnp.mean(x * x, axis=-1, keepdims=True)
    return x * lax.rsqrt(ms + EPS) * g


def _idiv(x, d):
    if d & (d - 1) == 0:
        return lax.shift_right_logical(x, int(math.log2(d)))
    return x // d


def _nt_dot(a, b):
    return lax.dot_general(a, b, (((1,), (1,)), ((), ())), preferred_element_type=F32)


def _ffn_body(x_ref, g_ref, wg_ref, wu_ref, wd_ref, o_ref, h_scr, acc_scr, *, tf):
    h_scr[...] = _rms_full(x_ref[...], g_ref[...]).astype(BF16)
    acc_scr[...] = jnp.zeros_like(acc_scr)

    def chunk(j, carry):
        c0 = pl.multiple_of(j * tf, tf)
        h = h_scr[...]
        gate = jnp.dot(h, wg_ref[:, pl.ds(c0, tf)], preferred_element_type=F32)
        up = jnp.dot(h, wu_ref[:, pl.ds(c0, tf)], preferred_element_type=F32)
        a = (gate * jax.nn.sigmoid(gate) * up).astype(BF16)
        acc_scr[...] += jnp.dot(a, wd_ref[pl.ds(c0, tf), :], preferred_element_type=F32)
        return carry

    lax.fori_loop(0, wg_ref.shape[1] // tf, chunk, 0)
    o_ref[...] = x_ref[...] + 0.5 * acc_scr[...]


def _ffn(x, g, wg, wu, wd, *, tm, tf):
    n, d = x.shape
    resident = lambda shape: pl.BlockSpec(shape, lambda i: (0, 0), pipeline_mode=pl.Buffered(1))
    return pl.pallas_call(
        functools.partial(_ffn_body, tf=tf),
        grid=(n // tm,),
        in_specs=[pl.BlockSpec((tm, d), lambda i: (i, 0)), resident((1, d)),
                  resident(wg.shape), resident(wu.shape), resident(wd.shape)],
        out_specs=pl.BlockSpec((tm, d), lambda i: (i, 0)),
        out_shape=jax.ShapeDtypeStruct((n, d), F32),
        scratch_shapes=[pltpu.VMEM((tm, d), BF16), pltpu.VMEM((tm, d), F32)],
        compiler_params=_cparams(("parallel",)),
        name="ffn",
    )(x, g, wg, wu, wd)


def _mixin_body(x_ref, g_ref, w_ref, wlr_ref, wup_ref, bg_ref, qn_ref, kn_ref, g64_ref,
                qg_ref, kg_ref, vg_ref, rg_ref, lf_ref, qd_ref, kd_ref, kdb_ref, vd_ref, vdb_ref,
                *, k_transposed):
    h = _rms_full(x_ref[...], g_ref[...]).astype(BF16)
    y = jnp.dot(h, w_ref[...], preferred_element_type=F32)
    qg_ref[...] = y[:, 0:256] * (DK_G ** -0.5)
    kg_ref[...] = y[:, 256:512]
    vg_ref[...] = y[:, 512:1024]
    rg_ref[...] = y[:, 1024:1536]
    qd = y[:, 1536:2048]
    kd = y[:, 2048:2560]
    vd = y[:, 2560:3072]

    glr = jnp.dot(h, wlr_ref[...], preferred_element_type=F32)
    z = jnp.dot(glr.astype(BF16), wup_ref[...], preferred_element_type=F32) + bg_ref[...]
    lf_ref[...] = (jnp.minimum(z, 0.0) - jnp.log(1.0 + jnp.exp(-jnp.abs(z)))) * (1.0 / GLA_GATE_NORM)

    def group_rms(t, gain):
        msq = jnp.dot((t * t).astype(BF16), g64_ref[...], preferred_element_type=F32) * (1.0 / DK_D)
        return t * lax.rsqrt(msq + EPS) * gain

    qdn = group_rms(qd, qn_ref[...]) * (DK_D ** -0.5 * LOG2E)
    kdn = group_rms(kd, kn_ref[...])
    qd_ref[...] = qdn.astype(BF16)
    if k_transposed:
        kd_ref[0] = kdn.T
    else:
        kd_ref[...] = kdn
    kdb_ref[...] = kdn.astype(BF16)
    for h in range(H_D):
        vd_ref[pl.ds(h, vd.shape[0], stride=H_D), :] = vd[:, h * DV_D:(h + 1) * DV_D]
    vdb_ref[...] = vd.astype(BF16)


def _mixin(x, g, w_main, w_lr, w_up, b_gate, qn, kn, g64, *, tm, seq=None):
    n, d = x.shape
    full = lambda shape: pl.BlockSpec(shape, lambda i: (0, 0))
    row = lambda w: pl.BlockSpec((tm, w), lambda i: (i, 0))
    outs = [(256, F32), (256, F32), (512, F32), (512, F32), (256, F32),
            (512, BF16), (512, F32), (512, BF16), (512, F32), (512, BF16)]
    out_specs = [row(w) for w, _ in outs]
    out_shape = [jax.ShapeDtypeStruct((n, w), dt) for w, dt in outs]
    if seq is not None:
        nt = seq // tm
        out_specs[6] = pl.BlockSpec((1, 512, tm), lambda i: (i // nt, 0, i % nt))
        out_shape[6] = jax.ShapeDtypeStruct((n // seq, 512, seq), F32)
    out_specs[8] = pl.BlockSpec((tm * H_D, DV_D), lambda i: (i, 0))
    out_shape[8] = jax.ShapeDtypeStruct((n * H_D, DV_D), F32)
    return pl.pallas_call(
        functools.partial(_mixin_body, k_transposed=seq is not None),
        grid=(n // tm,),
        in_specs=[row(d), full((1, d)), full(w_main.shape), full(w_lr.shape), full(w_up.shape),
                  full((1, 256)), full((1, 512)), full((1, 512)), full((512, 512))],
        out_specs=out_specs,
        out_shape=out_shape,
        compiler_params=_cparams(("parallel",)),
        name="mixer_in",
    )(x, g, w_main, w_lr, w_up, b_gate, qn, kn, g64)


def _gla_step(tb, n_tb, q_ref, k_ref, v_ref, lf_ref, s0_ref, tri_ref, g2_ref, o_ref, so_ref,
              st_scr, nat_scr, b_scr, *, C, n_chunks, unroll):
    @pl.when(tb == 0)
    def _():
        nat_scr[...] = jnp.zeros_like(nat_scr)
        for h in range(H_G):
            nat_scr[h * DK_G:(h + 1) * DK_G, h * DV_G:(h + 1) * DV_G] = s0_ref[0, h]
        st_scr[...] = nat_scr[...].T

    b_scr[...] = jnp.dot(tri_ref[...], lf_ref[0], preferred_element_type=F32,
                         precision=lax.Precision.HIGHEST)

    rows = lax.broadcasted_iota(jnp.int32, (H_G * DV_G, H_G * DK_G), 0)
    cols = lax.broadcasted_iota(jnp.int32, (H_G * DV_G, H_G * DK_G), 1)
    head_mask = _idiv(rows, DV_G) == _idiv(cols, DK_G)
    t_idx = lax.broadcasted_iota(jnp.int32, (C, H_G * DK_G), 0)

    def chunk(c, carry):
        r0 = pl.multiple_of(c * C, C)
        q = q_ref[0, pl.ds(r0, C), :]
        k = k_ref[0, pl.ds(r0, C), :]
        v = v_ref[0, pl.ds(r0, C), :]
        b = b_scr[pl.ds(r0, C), :]
        st = st_scr[...]

        o = _nt_dot((q * jnp.exp(b)).astype(BF16), st.astype(BF16))

        pieces = []
        for s in range(C):
            rel = jnp.where(t_idx >= s, b - b[s:s + 1, :], NEG)
            pieces.append(q * k[s:s + 1, :] * jnp.exp(rel))
        w = jnp.concatenate(pieces, axis=0).astype(BF16)
        a = jnp.dot(w, g2_ref[...], preferred_element_type=F32)
        for s in range(C):
            o = o + a[s * C:(s + 1) * C, :] * v[s:s + 1, :]
        o_ref[0, pl.ds(r0, C), :] = o

        b_last = b[C - 1:C, :]
        kt = (k * jnp.exp(b_last - b)).astype(BF16)
        upd = lax.dot_general(v.astype(BF16), kt, (((0,), (0,)), ((), ())),
                              preferred_element_type=F32)
        st_scr[...] = st * jnp.exp(b_last) + jnp.where(head_mask, upd, 0.0)
        return carry

    lax.fori_loop(0, n_chunks, chunk, 0, unroll=unroll)

    @pl.when(tb == n_tb - 1)
    def _():
        nat = st_scr[...].T
        for h in range(H_G):
            so_ref[0, h] = nat[h * DK_G:(h + 1) * DK_G, h * DV_G:(h + 1) * DV_G]


def _gla_body(*refs, C, n_chunks):
    _gla_step(pl.program_id(1), pl.num_programs(1), *refs, C=C, n_chunks=n_chunks, unroll=min(2, n_chunks))


def _gla_consts(tb, C):
    tri = np.kron(np.eye(tb // C, dtype=np.float32), np.tril(np.ones((C, C), np.float32)))
    g2 = np.kron(np.eye(H_G, dtype=np.float32), np.ones((DK_G, DV_G), np.float32))
    return jnp.asarray(tri), jnp.asarray(g2, BF16)


def _gla_scratch(tb):
    return [pltpu.VMEM((H_G * DV_G, H_G * DK_G), F32), pltpu.VMEM((H_G * DK_G, H_G * DV_G), F32),
            pltpu.VMEM((tb, H_G * DK_G), F32)]


def _gla(q, k, v, lf, s0, *, tb):
    bsz, seq, _ = q.shape
    C = math.gcd(seq, GLA_CHUNK)
    tb = min(tb, seq)
    tri, g2 = _gla_consts(tb, C)
    tok = lambda w: pl.BlockSpec((1, tb, w), lambda b, t: (b, t, 0))
    st_spec = pl.BlockSpec((1, H_G, DK_G, DV_G), lambda b, t: (b, 0, 0, 0))
    return pl.pallas_call(
        functools.partial(_gla_body, C=C, n_chunks=tb // C),
        grid=(bsz, seq // tb),
        in_specs=[tok(256), tok(256), tok(512), tok(256), st_spec,
                  pl.BlockSpec((tb, tb), lambda b, t: (0, 0)),
                  pl.BlockSpec((256, 512), lambda b, t: (0, 0))],
        out_specs=[tok(512), st_spec],
        out_shape=[jax.ShapeDtypeStruct((bsz, seq, 512), F32),
                   jax.ShapeDtypeStruct((bsz, H_G, DK_G, DV_G), F32)],
        scratch_shapes=_gla_scratch(tb),
        compiler_params=_cparams(("parallel", "arbitrary")),
        name="gla",
    )(q, k, v, lf, s0, tri, g2)


def _t5_shifted_bias(n, tab_ref, h):
    nf = jnp.maximum(n, 1).astype(F32)
    large = T5_MAX_EXACT + (jnp.log(nf / T5_MAX_EXACT) / math.log(T5_MAX_DIST / T5_MAX_EXACT)
                            * (N_BUCKETS - T5_MAX_EXACT)).astype(jnp.int32)
    large = jnp.minimum(large, N_BUCKETS - 1)
    bucket = jnp.where(n < T5_MAX_EXACT, n, large)
    far = tab_ref[h * N_BUCKETS + N_BUCKETS - 1]
    val = jnp.zeros(n.shape, F32)
    for b in range(N_BUCKETS - 1):
        val = jnp.where(bucket == b, (tab_ref[h * N_BUCKETS + b] - far) * LOG2E, val)
    return val


def _bias_prompt_body(tab_ref, o_ref, *, T, RB):
    h = pl.program_id(0)
    t = pl.program_id(1)
    for r0 in range(0, T, RB):
        r = lax.broadcasted_iota(jnp.int32, (RB, T), 0) + r0
        c = lax.broadcasted_iota(jnp.int32, (RB, T), 1)
        n = r - c + t * T
        val = _t5_shifted_bias(jnp.maximum(n, 0), tab_ref, h)
        o_ref[0, 0, r0:r0 + RB, :] = jnp.where(n >= 0, val, NEG)


def _bias_prompt_tiles(tab, *, T):
    assert T >= T5_MAX_DIST
    return pl.pallas_call(
        functools.partial(_bias_prompt_body, T=T, RB=64),
        grid=(H_D, 2),
        in_specs=[pl.BlockSpec(memory_space=pltpu.SMEM)],
        out_specs=pl.BlockSpec((1, 1, T, T), lambda h, t: (h, t, 0, 0)),
        out_shape=jax.ShapeDtypeStruct((H_D, 2, T, T), F32),
        compiler_params=_cparams(("parallel", "parallel")),
        name="bias_prompt",
    )(tab)


def _bias_paged_body(tab_ref, o_ref, *, LQ):
    t = pl.program_id(0)
    rh = 2 * LQ
    r = lax.broadcasted_iota(jnp.int32, (rh, PAGE), 0)
    c = lax.broadcasted_iota(jnp.int32, (rh, PAGE), 1)
    n = (r - _idiv(r, LQ) * LQ) - c + (1 - t) * PAGE
    for h in range(H_D):
        val = _t5_shifted_bias(jnp.maximum(n, 0), tab_ref, h)
        o_ref[0, h * rh:(h + 1) * rh, :] = jnp.where(n >= 0, val, NEG)


def _bias_paged_tiles(tab, *, LQ):
    assert PAGE >= T5_MAX_DIST
    R = H_D * 2 * LQ
    return pl.pallas_call(
        functools.partial(_bias_paged_body, LQ=LQ),
        grid=(2,),
        in_specs=[pl.BlockSpec(memory_space=pltpu.SMEM)],
        out_specs=pl.BlockSpec((1, R, PAGE), lambda t: (t, 0, 0)),
        out_shape=jax.ShapeDtypeStruct((2, R, PAGE), F32),
        compiler_params=_cparams(("parallel",)),
        name="bias_paged",
    )(tab)


def _dattn_body(q_ref, k_ref, v_ref, bias_ref, o_ref, qm_scr, m_scr, l_scr, acc_scr, *, T):
    qi = pl.program_id(2)
    m_scr[...] = jnp.full_like(m_scr, NEG)
    l_scr[...] = jnp.zeros_like(l_scr)
    acc_scr[...] = jnp.zeros_like(acc_scr)
    q = q_ref[0]
    lane = lax.broadcasted_iota(jnp.int32, q.shape, 1)
    qm_scr[0] = jnp.where(lane < DK_D, q, jnp.zeros_like(q))
    qm_scr[1] = jnp.where(lane >= DK_D, q, jnp.zeros_like(q))

    def step(kj, r0, nr, bias):
        c0 = pl.multiple_of(kj * T, T)
        k = k_ref[0, pl.ds(c0, T), :]
        v = v_ref[0, pl.ds(c0, T), :]
        for mp in range(2):
            s = _nt_dot(qm_scr[mp, r0:r0 + nr, :], k)
            if bias is not None:
                s = s + bias
            m_prev = m_scr[mp, r0:r0 + nr, :]
            m_new = jnp.maximum(m_prev, jnp.max(s, axis=-1, keepdims=True))
            corr = jnp.exp2(m_prev - m_new)
            p = jnp.exp2(s - jnp.concatenate([m_new] * (T // LANES), axis=1))
            l_scr[mp, r0:r0 + nr, :] = l_scr[mp, r0:r0 + nr, :] * corr + jnp.sum(p, axis=-1, keepdims=True)
            acc_scr[mp, r0:r0 + nr, :] = (acc_scr[mp, r0:r0 + nr, :] * corr
                                          + jnp.dot(p.astype(BF16), v, preferred_element_type=F32))
            m_scr[mp, r0:r0 + nr, :] = m_new

    def far_step(kj, carry):
        step(kj, 0, 2 * T, None)
        return carry

    lax.fori_loop(0, jnp.maximum(2 * qi - 1, 0), far_step, 0)

    diag = bias_ref[0, 0]
    sub = bias_ref[0, 1]

    @pl.when(qi >= 1)
    def _():
        step(2 * qi - 1, 0, 2 * T, jnp.concatenate([sub, jnp.zeros_like(sub)], axis=0))

    step(2 * qi, 0, 2 * T, jnp.concatenate([diag, sub], axis=0))
    step(2 * qi + 1, T, T, diag)
    for mp in range(2):
        o_ref[0, :, mp * DV_D:(mp + 1) * DV_D] = acc_scr[mp] / l_scr[mp]


def _dattn_prompt(q, k, v, bias_tiles, *, T):
    bsz, seq, _ = q.shape
    tq = 2 * T
    kv_spec = pl.BlockSpec((1, seq, 128), lambda b, h, qi: (b, 0, h))
    return pl.pallas_call(
        functools.partial(_dattn_body, T=T),
        grid=(bsz, H_D, seq // tq),
        in_specs=[pl.BlockSpec((1, tq, 128), lambda b, h, qi: (b, qi, h)), kv_spec, kv_spec,
                  pl.BlockSpec((1, 2, T, T), lambda b, h, qi: (h, 0, 0, 0))],
        out_specs=pl.BlockSpec((1, tq, 2 * DV_D), lambda b, h, qi: (b, qi, h)),
        out_shape=jax.ShapeDtypeStruct((bsz, seq, H_D * 2 * DV_D), F32),
        scratch_shapes=[pltpu.VMEM((2, tq, 128), BF16), pltpu.VMEM((2, tq, LANES), F32),
                        pltpu.VMEM((2, tq, LANES), F32), pltpu.VMEM((2, tq, DV_D), F32)],
        compiler_params=_cparams(("parallel", "parallel", "arbitrary")),
        name="diff_attn_prompt",
    )(q, k, v, bias_tiles)


def _paged_step(j, n_j, q_ref, kn_ref, vn_ref, bias_ref, k_refs, v_refs, o_ref,
                wq_scr, m_scr, l_scr, acc_scr, *, LQ):
    P = len(k_refs)
    RH = 2 * LQ

    def update(s, v_heads):
        n = s.shape[1] // PAGE
        m_prev = m_scr[...]
        m_new = jnp.maximum(m_prev, jnp.max(s, axis=-1, keepdims=True))
        corr = jnp.exp2(m_prev - m_new)
        p = jnp.exp2(s - jnp.concatenate([m_new] * n, axis=1))
        l_scr[...] = l_scr[...] * corr + jnp.sum(p, axis=-1, keepdims=True)
        pb = p.astype(BF16)
        pv = [jnp.dot(pb[h * RH:(h + 1) * RH, :], v_heads[h], preferred_element_type=F32)
              for h in range(H_D)]
        acc_scr[...] = acc_scr[...] * corr + jnp.concatenate(pv, axis=0)
        m_scr[...] = m_new

    @pl.when(j == 0)
    def _():
        q = q_ref[0]
        qt = jnp.concatenate([q] * (H_D * 2), axis=0)
        rr = lax.broadcasted_iota(jnp.int32, qt.shape, 0)
        cc = lax.broadcasted_iota(jnp.int32, qt.shape, 1)
        wq_scr[...] = jnp.where(_idiv(rr, LQ) == _idiv(cc, DK_D), qt, jnp.zeros_like(qt))
        m_scr[...] = jnp.full_like(m_scr, NEG)
        l_scr[...] = jnp.zeros_like(l_scr)
        acc_scr[...] = jnp.zeros_like(acc_scr)
        pad = jnp.zeros((PAGE - LQ, H_D * 2 * DK_D), BF16)
        kn = jnp.concatenate([kn_ref[0], pad], axis=0)
        vn = jnp.concatenate([vn_ref[0], pad], axis=0)
        update(_nt_dot(wq_scr[...], kn) + bias_ref[1],
               [vn[:, h * DV_D:(h + 1) * DV_D] for h in range(H_D)])

    is_last = j == n_j - 1
    kt = jnp.concatenate([k_refs[i][0].astype(BF16) for i in range(P)], axis=1)
    s = jnp.dot(wq_scr[...], kt, preferred_element_type=F32)
    last_bias = bias_ref[0] * jnp.where(is_last, 1.0, 0.0)
    s = jnp.concatenate([s[:, :(P - 1) * PAGE], s[:, (P - 1) * PAGE:] + last_bias], axis=1)
    v_heads = [jnp.concatenate([v_refs[i][0, pl.ds(h, PAGE, stride=H_D), :].astype(BF16)
                                for i in range(P)], axis=0) for h in range(H_D)]
    update(s, v_heads)

    @pl.when(is_last)
    def _():
        o_ref[0] = acc_scr[...] / l_scr[...]


def _paged_gla_body(pt_ref, *refs, P, LQ, n_j, n_tb, C, n_chunks):
    q_ref, kn_ref, vn_ref, bias_ref = refs[:4]
    k_refs = refs[4:4 + P]
    v_refs = refs[4 + P:4 + 2 * P]
    gla_in = refs[4 + 2 * P:4 + 2 * P + 7]
    o_ref, go_ref, gs_ref = refs[4 + 2 * P + 7:4 + 2 * P + 10]
    paged_scr = refs[4 + 2 * P + 10:4 + 2 * P + 14]
    gla_scr = refs[4 + 2 * P + 14:]
    s = pl.program_id(0)
    _gla_step(lax.rem(s, n_tb), n_tb, *gla_in, go_ref, gs_ref, *gla_scr, C=C, n_chunks=n_chunks,
              unroll=n_chunks)
    _paged_step(lax.rem(s, n_j), n_j, q_ref, kn_ref, vn_ref, bias_ref, k_refs, v_refs, o_ref,
                *paged_scr, LQ=LQ)


def _paged_attn_with_gla(q, k_new, v_new, kt_pool, v_pool, page_table, bias_tiles,
                         gq, gk, gv, glf, gs0, *, P):
    bd, lq, _ = q.shape
    n_pages = page_table.shape[1]
    n_j = n_pages // P
    steps = bd * n_j
    bsz, seq, _ = gq.shape
    C = math.gcd(seq, GLA_CHUNK)
    tb = bsz * seq // steps
    assert tb * steps == bsz * seq and tb % C == 0 and seq % tb == 0
    n_tb = seq // tb
    tri, g2 = _gla_consts(tb, C)
    R = H_D * 2 * lq
    tok = pl.BlockSpec((1, lq, 512), lambda s, pt: (s // n_j, 0, 0))
    const = lambda shape: pl.BlockSpec(shape, lambda s, pt: (0,) * len(shape))

    def page_spec(i):
        return pl.BlockSpec((1, 512, PAGE), lambda s, pt: (pt[(s // n_j) * n_pages + (s % n_j) * P + i], 0, 0))

    gtok = lambda w: pl.BlockSpec((1, tb, w), lambda s, pt: (s // n_tb, s % n_tb, 0))
    gst = pl.BlockSpec((1, H_G, DK_G, DV_G), lambda s, pt: (s // n_tb, 0, 0, 0))
    grid_spec = pltpu.PrefetchScalarGridSpec(
        num_scalar_prefetch=1,
        grid=(steps,),
        in_specs=[tok, tok, tok, const((2, R, PAGE))] + [page_spec(i) for i in range(P)] * 2
        + [gtok(256), gtok(256), gtok(512), gtok(256), gst, const((tb, tb)), const((256, 512))],
        out_specs=[pl.BlockSpec((1, R, DV_D), lambda s, pt: (s // n_j, 0, 0)), gtok(512), gst],
        scratch_shapes=[pltpu.VMEM((R, 512), BF16), pltpu.VMEM((R, LANES), F32),
                        pltpu.VMEM((R, LANES), F32), pltpu.VMEM((R, DV_D), F32)] + _gla_scratch(tb),
    )
    return pl.pallas_call(
        functools.partial(_paged_gla_body, P=P, LQ=lq, n_j=n_j, n_tb=n_tb, C=C, n_chunks=tb // C),
        grid_spec=grid_spec,
        out_shape=[jax.ShapeDtypeStruct((bd, R, DV_D), F32),
                   jax.ShapeDtypeStruct((bsz, seq, 512), F32),
                   jax.ShapeDtypeStruct((bsz, H_G, DK_G, DV_G), F32)],
        compiler_params=_cparams(("arbitrary",)),
        name="paged_attn_gla",
    )(page_table.reshape(-1), q, k_new, v_new, bias_tiles, *([kt_pool] * P), *([v_pool] * P),
      gq, gk, gv, glf, gs0, tri, g2)


def _mixout_math(x, og, rg, od, lam4, gn, sn, wo, lam_init):
    lam = (jnp.exp(jnp.sum(lam4[0:1, :] * lam4[1:2, :], axis=-1, keepdims=True))
           - jnp.exp(jnp.sum(lam4[2:3, :] * lam4[3:4, :], axis=-1, keepdims=True)) + lam_init)
    parts = []
    for h in range(H_G):
        t = og[:, h * DV_G:(h + 1) * DV_G]
        r = rg[:, h * DV_G:(h + 1) * DV_G]
        parts.append(_rms_full(t, gn) * (r * jax.nn.sigmoid(r)))
    for h in range(H_D):
        d = od[:, h * 2 * DV_D:h * 2 * DV_D + DV_D] - lam * od[:, h * 2 * DV_D + DV_D:(h + 1) * 2 * DV_D]
        parts.append(_rms_full(d, sn) * (1.0 - lam_init))
    cat = jnp.concatenate(parts, axis=-1).astype(BF16)
    return x + jnp.dot(cat, wo, preferred_element_type=F32)


def _mixout_body(x_ref, og_ref, rg_ref, od_ref, lam4_ref, gn_ref, sn_ref, wo_ref, o_ref, *, lam_init):
    o_ref[...] = _mixout_math(x_ref[...], og_ref[...], rg_ref[...], od_ref[...], lam4_ref[...],
                              gn_ref[...], sn_ref[...], wo_ref[...], lam_init)


def _mix_xattn_body(x_ref, og_ref, rg_ref, od_ref, lam4_ref, gn_ref, sn_ref, wo_ref,
                    xg_ref, wq_ref, qn_ref, mk_ref, mv_ref, wox_ref, o_ref, *, lam_init, dh):
    x = _mixout_math(x_ref[...], og_ref[...], rg_ref[...], od_ref[...], lam4_ref[...],
                     gn_ref[...], sn_ref[...], wo_ref[...], lam_init)
    h = _rms_full(x, xg_ref[...]).astype(BF16)
    y = jnp.dot(h, wq_ref[...], preferred_element_type=F32)
    qn = qn_ref[...]
    k = mk_ref[0]
    v = mv_ref[0]
    outs = []
    for i in range(H_X):
        sl = slice(i * dh, (i + 1) * dh)
        q = (_rms_full(y[:, sl], qn) * (dh ** -0.5)).astype(BF16)
        s = _nt_dot(q, k[:, sl])
        p = jnp.exp(s - jnp.max(s, axis=-1, keepdims=True))
        p = p / jnp.sum(p, axis=-1, keepdims=True)
        outs.append(jnp.dot(p.astype(BF16), v[:, sl], preferred_element_type=F32))
    att = jnp.concatenate(outs, axis=-1).astype(BF16)
    o_ref[...] = x + jnp.dot(att, wox_ref[...], preferred_element_type=F32)


def _mix_xattn(x, og, rg, od, lam4, gn, sn, wo, xg, wq, qn, mk, mv, wox, *, tm, seq, lam_init):
    n, d = x.shape
    nt = seq // tm
    resident = lambda shape: pl.BlockSpec(shape, lambda i: (0,) * len(shape), pipeline_mode=pl.Buffered(1))
    row = lambda w: pl.BlockSpec((tm, w), lambda i: (i, 0))
    mem = pl.BlockSpec((1,) + mk.shape[1:], lambda i: (i // nt, 0, 0))
    return pl.pallas_call(
        functools.partial(_mix_xattn_body, lam_init=lam_init, dh=d // H_X),
        grid=(n // tm,),
        in_specs=[row(d), row(512), row(512), row(1024), resident((4, DK_D)), resident((1, DV_G)),
                  resident((1, DV_D)), resident(wo.shape), resident((1, d)), resident(wq.shape),
                  resident(qn.shape), mem, mem, resident(wox.shape)],
        out_specs=row(d),
        out_shape=jax.ShapeDtypeStruct((n, d), F32),
        compiler_params=_cparams(("parallel",)),
        name="mixer_out_xattn",
    )(x, og, rg, od, lam4, gn, sn, wo, xg, wq, qn, mk, mv, wox)


def _mixout(x, og, rg, od, lam4, gn, sn, wo, *, tm, lam_init):
    n, d = x.shape
    full = lambda shape: pl.BlockSpec(shape, lambda i: (0, 0))
    row = lambda w: pl.BlockSpec((tm, w), lambda i: (i, 0))
    return pl.pallas_call(
        functools.partial(_mixout_body, lam_init=lam_init),
        grid=(n // tm,),
        in_specs=[row(d), row(512), row(512), row(1024), full((4, DK_D)), full((1, DV_G)),
                  full((1, DV_D)), full(wo.shape)],
        out_specs=row(d),
        out_shape=jax.ShapeDtypeStruct((n, d), F32),
        compiler_params=_cparams(("parallel",)),
        name="mixer_out",
    )(x, og, rg, od, lam4, gn, sn, wo)


def _proj_body(x_ref, g_ref, w_ref, hn_ref, *o_refs, head_norm, scale, dh):
    h = _rms_full(x_ref[...], g_ref[...]).astype(BF16)
    y = jnp.dot(h, w_ref[...], preferred_element_type=F32)
    if head_norm:
        hn = hn_ref[...]
        y = jnp.concatenate(
            [_rms_full(y[:, i * dh:(i + 1) * dh], hn) for i in range(y.shape[1] // dh)], axis=-1)
    y = y * scale
    for o_ref in o_refs:
        o_ref[...] = y.astype(o_ref.dtype)


def _proj(x, g, w, hn, *, tm, head_norm, scale, out_dtypes):
    n, d = x.shape
    dout = w.shape[1]
    dh = hn.shape[1]
    full = lambda shape: pl.BlockSpec(shape, lambda i: (0, 0))
    row = lambda wd: pl.BlockSpec((tm, wd), lambda i: (i, 0))
    return pl.pallas_call(
        functools.partial(_proj_body, head_norm=head_norm, scale=scale, dh=dh),
        grid=(n // tm,),
        in_specs=[row(d), full((1, d)), full(w.shape), full((1, dh))],
        out_specs=[row(dout) for _ in out_dtypes],
        out_shape=[jax.ShapeDtypeStruct((n, dout), dt) for dt in out_dtypes],
        compiler_params=_cparams(("parallel",)),
        name="proj",
    )(x, g, w, hn)


def _xattn_cached_body(q_ref, k_ref, v_ref, o_ref, *, dh, M):
    q = q_ref[0]
    nlb = dh // LANES
    stride = nlb * H_X
    lq = q.shape[0]
    s_heads = []
    for h in range(H_X):
        s = None
        for lb in range(nlb):
            kh = k_ref[0, pl.ds(lb * H_X + h, M, stride=stride), :].astype(BF16)
            t = _nt_dot(q[:, h * dh + lb * LANES:h * dh + (lb + 1) * LANES], kh)
            s = t if s is None else s + t
        s_heads.append(s)
    s = jnp.concatenate(s_heads, axis=0)
    m = jnp.max(s, axis=-1, keepdims=True)
    p = jnp.exp(s - m)
    p = p / jnp.sum(p, axis=-1, keepdims=True)
    outs = []
    for h in range(H_X):
        ph = p[h * lq:(h + 1) * lq, :].astype(BF16)
        for lb in range(nlb):
            vh = v_ref[0, pl.ds(lb * H_X + h, M, stride=stride), :].astype(BF16)
            outs.append(jnp.dot(ph, vh, preferred_element_type=F32))
    o_ref[0] = jnp.concatenate(outs, axis=-1).astype(o_ref.dtype)


def _xattn_cached(q, k_rows, v_rows, *, M):
    bsz, lq, d = q.shape
    kv = pl.BlockSpec((1, k_rows.shape[1], LANES), lambda b: (b, 0, 0))
    return pl.pallas_call(
        functools.partial(_xattn_cached_body, dh=d // H_X, M=M),
        grid=(bsz,),
        in_specs=[pl.BlockSpec((1, lq, d), lambda b: (b, 0, 0)), kv, kv],
        out_specs=pl.BlockSpec((1, lq, d), lambda b: (b, 0, 0)),
        out_shape=jax.ShapeDtypeStruct((bsz, lq, d), BF16),
        compiler_params=_cparams(("parallel",)),
        name="xattn_cached",
    )(q, k_rows, v_rows)


def _resproj_body(x_ref, a_ref, w_ref, o_ref):
    o_ref[...] = x_ref[...] + jnp.dot(a_ref[...], w_ref[...], preferred_element_type=F32)


def _resproj(x, a, w, *, tm):
    n, d = x.shape
    row = lambda wd: pl.BlockSpec((tm, wd), lambda i: (i, 0))
    return pl.pallas_call(
        _resproj_body,
        grid=(n // tm,),
        in_specs=[row(d), row(a.shape[1]), pl.BlockSpec(w.shape, lambda i: (0, 0))],
        out_specs=row(d),
        out_shape=jax.ShapeDtypeStruct((n, d), F32),
        compiler_params=_cparams(("parallel",)),
        name="resproj",
    )(x, a, w)


def kernel(x_prompt, x_sample, mem_prompt, cache_diff_k, cache_diff_v, state_gla, cache_mem_k, cache_mem_v, page_table, rel_bias, ffn1_norm, ffn1_w_gate, ffn1_w_up, ffn1_w_down, mix_norm, w_in, gla_w_gate_up, gla_b_gate, gla_out_norm, diff_q_norm, diff_k_norm, diff_lam_q1, diff_lam_k1, diff_lam_q2, diff_lam_k2, diff_subln, w_out, xattn_norm, mem_norm, xattn_wq, xattn_wk, xattn_wv, xattn_q_norm, xattn_k_norm, xattn_wo, ffn2_norm, ffn2_w_gate, ffn2_w_up, ffn2_w_down):
    depth = ffn1_norm.shape[0]
    B, L, D = x_prompt.shape
    Bd, Ld, _ = x_sample.shape
    M = mem_prompt.shape[1]
    n_pool = cache_diff_k.shape[1]
    T = 512
    TM = 512
    P_PAGES = 32

    xp = x_prompt.reshape(B * L, D)
    xs = x_sample.reshape(Bd * Ld, D)
    outs = {k: [] for k in ("kp", "vp", "ks", "vs", "sp", "ss", "mk", "mv")}

    g64 = jnp.asarray(np.kron(np.eye(512 // DK_D, dtype=np.float32), np.ones((DK_D, DK_D), np.float32)), BF16)

    for l in range(depth):
        lam_init = 0.8 - 0.6 * math.exp(-0.3 * l)
        row = lambda a: a[l].reshape(1, -1)

        w1g, w1u, w1d = (w[l].astype(BF16) for w in (ffn1_w_gate, ffn1_w_up, ffn1_w_down))
        w2g, w2u, w2d = (w[l].astype(BF16) for w in (ffn2_w_gate, ffn2_w_up, ffn2_w_down))
        wi = w_in[l]
        lr0 = 2 * H_G * DK_G + 2 * H_G * DV_G
        w_main = jnp.concatenate([wi[:, :lr0], wi[:, lr0 + GLA_RANK:]], axis=1).astype(BF16)
        w_lr = jnp.pad(wi[:, lr0:lr0 + GLA_RANK], ((0, 0), (0, 128 - GLA_RANK))).astype(BF16)
        w_up = jnp.pad(gla_w_gate_up[l], ((0, 128 - GLA_RANK), (0, 0))).astype(BF16)
        qn = jnp.tile(diff_q_norm[l], 2 * H_D).reshape(1, -1)
        kn = jnp.tile(diff_k_norm[l], 2 * H_D).reshape(1, -1)
        lam4 = jnp.stack([diff_lam_q1[l], diff_lam_k1[l], diff_lam_q2[l], diff_lam_k2[l]]).astype(F32)
        wo_mix = w_out[l].astype(BF16)
        wq_x, wk_x, wv_x, wo_x = (w[l].astype(BF16) for w in (xattn_wq, xattn_wk, xattn_wv, xattn_wo))

        tab = rel_bias.astype(F32).T.reshape(-1)
        bias_tiles = _bias_prompt_tiles(tab, T=T)
        bias_paged = _bias_paged_tiles(tab, LQ=Ld)

        xp = _ffn(xp, row(ffn1_norm), w1g, w1u, w1d, tm=1024, tf=256)
        xs = _ffn(xs, row(ffn1_norm), w1g, w1u, w1d, tm=1024, tf=256)

        mix_args = (row(mix_norm), w_main, w_lr, w_up, row(gla_b_gate), qn, kn, g64)
        qg, kg, vg, rg, lf, qd, kdt, kdb, vd, vdb = _mixin(xp, *mix_args, tm=TM, seq=L)
        qg_s, kg_s, vg_s, rg_s, lf_s, qd_s, kd_s, kdb_s, vd_s, vdb_s = _mixin(xs, *mix_args, tm=TM)
        b3 = lambda a: a.reshape(B, L, -1)
        s3 = lambda a: a.reshape(Bd, Ld, -1)
        og_s, ss = _gla(s3(qg_s), s3(kg_s), s3(vg_s), s3(lf_s), state_gla[l], tb=Ld)

        kt_pool = jnp.transpose(cache_diff_k[l], (0, 2, 3, 4, 1)).reshape(n_pool, H_D * 2 * DK_D, PAGE)
        v_pool = cache_diff_v[l].reshape(n_pool, PAGE * H_D, DV_D)
        s0 = jnp.zeros((B, H_G, DK_G, DV_G), F32)
        od_s, og, sp = _paged_attn_with_gla(s3(qd_s), s3(kdb_s), s3(vdb_s), kt_pool, v_pool, page_table,
                                            bias_paged, b3(qg), b3(kg), b3(vg), b3(lf), s0, P=P_PAGES)

        od = _dattn_prompt(b3(qd), b3(kdb), b3(vdb), bias_tiles, T=T)
        dh = D // H_X
        mem2 = mem_prompt.reshape(B * M, D)
        mk, mkb = _proj(mem2, row(mem_norm), wk_x, row(xattn_k_norm), tm=TM, head_norm=True, scale=1.0,
                        out_dtypes=(F32, BF16))
        mv, mvb = _proj(mem2, row(mem_norm), wv_x, jnp.ones((1, dh), F32), tm=TM, head_norm=False,
                        scale=1.0, out_dtypes=(F32, BF16))
        xp = _mix_xattn(xp, og.reshape(B * L, -1), rg, od.reshape(B * L, -1), lam4, row(gla_out_norm),
                        row(diff_subln), wo_mix, row(xattn_norm), wq_x, row(xattn_q_norm),
                        mkb.reshape(B, M, D), mvb.reshape(B, M, D), wo_x, tm=TM, seq=L, lam_init=lam_init)
        outs["mk"].append(mk.reshape(B, M, H_X, dh))
        outs["mv"].append(mv.reshape(B, M, H_X, dh))
        outs["kp"].append(kdt.reshape(B, H_D, 2, DK_D, L).transpose(0, 4, 1, 2, 3))
        outs["vp"].append(vd.reshape(B, L, H_D, DV_D))
        outs["sp"].append(sp)

        od_s = od_s.reshape(Bd, H_D, 2, Ld, DV_D).transpose(0, 3, 1, 2, 4).reshape(Bd * Ld, -1)
        xs = _mixout(xs, og_s.reshape(Bd * Ld, -1), rg_s, od_s, lam4, row(gla_out_norm), row(diff_subln),
                     wo_mix, tm=TM, lam_init=lam_init)
        outs["ks"].append(kd_s.reshape(Bd, Ld, H_D, 2, DK_D))
        outs["vs"].append(vd_s.reshape(Bd, Ld, H_D, DV_D))
        outs["ss"].append(ss)

        nlb = dh // LANES
        (qs,) = _proj(xs, row(xattn_norm), wq_x, row(xattn_q_norm), tm=TM, head_norm=True,
                      scale=dh ** -0.5, out_dtypes=(BF16,))
        cache_rows = lambda c: (c[l].reshape(Bd, M, H_X, nlb, LANES).transpose(0, 1, 3, 2, 4)
                                .reshape(Bd, M * nlb * H_X, LANES))
        osm = _xattn_cached(qs.reshape(Bd, Ld, D), cache_rows(cache_mem_k), cache_rows(cache_mem_v), M=M)
        xs = _resproj(xs, osm.reshape(Bd * Ld, D), wo_x, tm=TM)

        xp = _ffn(xp, row(ffn2_norm), w2g, w2u, w2d, tm=1024, tf=256)
        xs = _ffn(xs, row(ffn2_norm), w2g, w2u, w2d, tm=1024, tf=256)

    st = lambda key: jnp.stack(outs[key])
    return (xp.reshape(B, L, D), xs.reshape(Bd, Ld, D), st("kp"), st("vp"), st("ks"), st("vs"),
            st("sp"), st("ss"), st("mk"), st("mv"))
```

```python
import functools
import math

import numpy as np
import jax
import jax.numpy as jnp
from jax import lax
from jax.experimental import pallas as pl
from jax.experimental.pallas import tpu as pltpu

F32 = jnp.float32
BF16 = jnp.bfloat16

EPS = 1e-6
NEG = -1e30
LOG2E = 1.0 / math.log(2.0)

H_G, DK_G, DV_G = 4, 64, 128
GLA_RANK = 16
GLA_GATE_NORM = 16.0
GLA_CHUNK = 16
H_D, DK_D, DV_D = 4, 64, 128
N_BUCKETS = 32
T5_MAX_EXACT = N_BUCKETS // 2
T5_MAX_DIST = 128
H_X = 4
PAGE = 128
LANES = 128

VMEM_LIMIT = 56 * 1024 * 1024


def _cparams(sem):
    return pltpu.CompilerParams(dimension_semantics=sem, vmem_limit_bytes=VMEM_LIMIT)


def _rms_full(x, g):
    ms = jnp.mean(x * x, axis=-1, keepdims=True)
    return x * lax.rsqrt(ms + EPS) * g


def _idiv(x, d):
    if d & (d - 1) == 0:
        return lax.shift_right_logical(x, int(math.log2(d)))
    return x // d


def _nt_dot(a, b):
    return lax.dot_general(a, b, (((1,), (1,)), ((), ())), preferred_element_type=F32)


def _ffn_body(x_ref, g_ref, wg_ref, wu_ref, wd_ref, o_ref, h_scr, acc_scr, *, tf):
    h_scr[...] = _rms_full(x_ref[...], g_ref[...]).astype(BF16)
    acc_scr[...] = jnp.zeros_like(acc_scr)

    def chunk(j, carry):
        c0 = pl.multiple_of(j * tf, tf)
        h = h_scr[...]
        gate = jnp.dot(h, wg_ref[:, pl.ds(c0, tf)], preferred_element_type=F32)
        up = jnp.dot(h, wu_ref[:, pl.ds(c0, tf)], preferred_element_type=F32)
        a = (gate * jax.nn.sigmoid(gate) * up).astype(BF16)
        acc_scr[...] += jnp.dot(a, wd_ref[pl.ds(c0, tf), :], preferred_element_type=F32)
        return carry

    lax.fori_loop(0, wg_ref.shape[1] // tf, chunk, 0)
    o_ref[...] = x_ref[...] + 0.5 * acc_scr[...]


def _ffn(x, g, wg, wu, wd, *, tm, tf):
    n, d = x.shape
    resident = lambda shape: pl.BlockSpec(shape, lambda i: (0, 0), pipeline_mode=pl.Buffered(1))
    return pl.pallas_call(
        functools.partial(_ffn_body, tf=tf),
        grid=(n // tm,),
        in_specs=[pl.BlockSpec((tm, d), lambda i: (i, 0)), resident((1, d)),
                  resident(wg.shape), resident(wu.shape), resident(wd.shape)],
        out_specs=pl.BlockSpec((tm, d), lambda i: (i, 0)),
        out_shape=jax.ShapeDtypeStruct((n, d), F32),
        scratch_shapes=[pltpu.VMEM((tm, d), BF16), pltpu.VMEM((tm, d), F32)],
        compiler_params=_cparams(("parallel",)),
        name="ffn",
    )(x, g, wg, wu, wd)


def _mixin_body(x_ref, g_ref, w_ref, wlr_ref, wup_ref, bg_ref, qn_ref, kn_ref, g64_ref,
                qg_ref, kg_ref, vg_ref, rg_ref, lf_ref, qd_ref, kd_ref, kdb_ref, vd_ref, vdb_ref,
                *, k_transposed):
    h = _rms_full(x_ref[...], g_ref[...]).astype(BF16)
    y = jnp.dot(h, w_ref[...], preferred_element_type=F32)
    qg_ref[...] = y[:, 0:256] * (DK_G ** -0.5)
    kg_ref[...] = y[:, 256:512]
    vg_ref[...] = y[:, 512:1024]
    rg_ref[...] = y[:, 1024:1536]
    qd = y[:, 1536:2048]
    kd = y[:, 2048:2560]
    vd = y[:, 2560:3072]

    glr = jnp.dot(h, wlr_ref[...], preferred_element_type=F32)
    z = jnp.dot(glr.astype(BF16), wup_ref[...], preferred_element_type=F32) + bg_ref[...]
    lf_ref[...] = (jnp.minimum(z, 0.0) - jnp.log(1.0 + jnp.exp(-jnp.abs(z)))) * (1.0 / GLA_GATE_NORM)

    def group_rms(t, gain):
        msq = jnp.dot((t * t).astype(BF16), g64_ref[...], preferred_element_type=F32) * (1.0 / DK_D)
        return t * lax.rsqrt(msq + EPS) * gain

    qdn = group_rms(qd, qn_ref[...]) * (DK_D ** -0.5 * LOG2E)
    kdn = group_rms(kd, kn_ref[...])
    qd_ref[...] = qdn.astype(BF16)
    if k_transposed:
        kd_ref[0] = kdn.T
    else:
        kd_ref[...] = kdn
    kdb_ref[...] = kdn.astype(BF16)
    for h in range(H_D):
        vd_ref[pl.ds(h, vd.shape[0], stride=H_D), :] = vd[:, h * DV_D:(h + 1) * DV_D]
    vdb_ref[...] = vd.astype(BF16)


def _mixin(x, g, w_main, w_lr, w_up, b_gate, qn, kn, g64, *, tm, seq=None):
    n, d = x.shape
    full = lambda shape: pl.BlockSpec(shape, lambda i: (0, 0))
    row = lambda w: pl.BlockSpec((tm, w), lambda i: (i, 0))
    outs = [(256, F32), (256, F32), (512, F32), (512, F32), (256, F32),
            (512, BF16), (512, F32), (512, BF16), (512, F32), (512, BF16)]
    out_specs = [row(w) for w, _ in outs]
    out_shape = [jax.ShapeDtypeStruct((n, w), dt) for w, dt in outs]
    if seq is not None:
        nt = seq // tm
        out_specs[6] = pl.BlockSpec((1, 512, tm), lambda i: (i // nt, 0, i % nt))
        out_shape[6] = jax.ShapeDtypeStruct((n // seq, 512, seq), F32)
    out_specs[8] = pl.BlockSpec((tm * H_D, DV_D), lambda i: (i, 0))
    out_shape[8] = jax.ShapeDtypeStruct((n * H_D, DV_D), F32)
    return pl.pallas_call(
        functools.partial(_mixin_body, k_transposed=seq is not None),
        grid=(n // tm,),
        in_specs=[row(d), full((1, d)), full(w_main.shape), full(w_lr.shape), full(w_up.shape),
                  full((1, 256)), full((1, 512)), full((1, 512)), full((512, 512))],
        out_specs=out_specs,
        out_shape=out_shape,
        compiler_params=_cparams(("parallel",)),
        name="mixer_in",
    )(x, g, w_main, w_lr, w_up, b_gate, qn, kn, g64)


def _gla_phases(tb, n_tb, q_ref, k_ref, v_ref, lf_ref, s0_ref, tri_ref, g2_ref, o_ref, so_ref,
                st_scr, nat_scr, b_scr, *, C, n_chunks, unroll):
    def init():
        @pl.when(tb == 0)
        def _():
            nat_scr[...] = jnp.zeros_like(nat_scr)
            for h in range(H_G):
                nat_scr[h * DK_G:(h + 1) * DK_G, h * DV_G:(h + 1) * DV_G] = s0_ref[0, h]
            st_scr[...] = nat_scr[...].T

    def final():
        @pl.when(tb == n_tb - 1)
        def _():
            nat = st_scr[...].T
            for h in range(H_G):
                so_ref[0, h] = nat[h * DK_G:(h + 1) * DK_G, h * DV_G:(h + 1) * DV_G]

    def chunk(c, carry):
        rows = lax.broadcasted_iota(jnp.int32, (H_G * DV_G, H_G * DK_G), 0)
        cols = lax.broadcasted_iota(jnp.int32, (H_G * DV_G, H_G * DK_G), 1)
        head_mask = _idiv(rows, DV_G) == _idiv(cols, DK_G)
        t_idx = lax.broadcasted_iota(jnp.int32, (C, H_G * DK_G), 0)
        r0 = pl.multiple_of(c * C, C)
        q = q_ref[0, pl.ds(r0, C), :]
        k = k_ref[0, pl.ds(r0, C), :]
        v = v_ref[0, pl.ds(r0, C), :]
        b = b_scr[pl.ds(r0, C), :]
        st = st_scr[...]

        o = _nt_dot((q * jnp.exp(b)).astype(BF16), st.astype(BF16))

        pieces = []
        for s in range(C):
            rel = jnp.where(t_idx >= s, b - b[s:s + 1, :], NEG)
            pieces.append(q * k[s:s + 1, :] * jnp.exp(rel))
        w = jnp.concatenate(pieces, axis=0).astype(BF16)
        a = jnp.dot(w, g2_ref[...], preferred_element_type=F32)
        for s in range(C):
            o = o + a[s * C:(s + 1) * C, :] * v[s:s + 1, :]
        o_ref[0, pl.ds(r0, C), :] = o

        b_last = b[C - 1:C, :]
        kt = (k * jnp.exp(b_last - b)).astype(BF16)
        upd = lax.dot_general(v.astype(BF16), kt, (((0,), (0,)), ((), ())),
                              preferred_element_type=F32)
        st_scr[...] = st * jnp.exp(b_last) + jnp.where(head_mask, upd, 0.0)
        return carry

    def main():
        b_scr[...] = jnp.dot(tri_ref[...], lf_ref[0], preferred_element_type=F32,
                             precision=lax.Precision.HIGHEST)
        lax.fori_loop(0, n_chunks, chunk, 0, unroll=unroll)

    return init, main, final


def _gla_body(*refs, C, n_chunks, G):
    q_ref, k_ref, v_ref, lf_ref, s0_ref, tri_ref, g2_ref, o_ref, so_ref = refs[:9]
    scratch = refs[9:]
    per_seq = []
    for g in range(G):
        one = lambda r, g=g: r.at[pl.ds(g, 1)]
        per_seq.append(_gla_phases(
            pl.program_id(1), pl.num_programs(1), one(q_ref), one(k_ref), one(v_ref), one(lf_ref),
            one(s0_ref), tri_ref, g2_ref, one(o_ref), one(so_ref), *[s.at[g] for s in scratch],
            C=C, n_chunks=n_chunks, unroll=min(2, n_chunks)))
    for phases in zip(*per_seq):
        for phase in phases:
            phase()


def _gla_consts(tb, C):
    tri = np.kron(np.eye(tb // C, dtype=np.float32), np.tril(np.ones((C, C), np.float32)))
    g2 = np.kron(np.eye(H_G, dtype=np.float32), np.ones((DK_G, DV_G), np.float32))
    return jnp.asarray(tri), jnp.asarray(g2, BF16)


def _gla_scratch(tb, lead=()):
    return [pltpu.VMEM(lead + (H_G * DV_G, H_G * DK_G), F32), pltpu.VMEM(lead + (H_G * DK_G, H_G * DV_G), F32),
            pltpu.VMEM(lead + (tb, H_G * DK_G), F32)]


def _gla(q, k, v, lf, s0, *, tb, G):
    bsz, seq, _ = q.shape
    C = math.gcd(seq, GLA_CHUNK)
    tb = min(tb, seq)
    tri, g2 = _gla_consts(tb, C)
    tok = lambda w: pl.BlockSpec((G, tb, w), lambda b, t: (b, t, 0))
    st_spec = pl.BlockSpec((G, H_G, DK_G, DV_G), lambda b, t: (b, 0, 0, 0))
    return pl.pallas_call(
        functools.partial(_gla_body, C=C, n_chunks=tb // C, G=G),
        grid=(bsz // G, seq // tb),
        in_specs=[tok(256), tok(256), tok(512), tok(256), st_spec,
                  pl.BlockSpec((tb, tb), lambda b, t: (0, 0)),
                  pl.BlockSpec((256, 512), lambda b, t: (0, 0))],
        out_specs=[tok(512), st_spec],
        out_shape=[jax.ShapeDtypeStruct((bsz, seq, 512), F32),
                   jax.ShapeDtypeStruct((bsz, H_G, DK_G, DV_G), F32)],
        scratch_shapes=_gla_scratch(tb, (G,)),
        compiler_params=_cparams(("parallel", "arbitrary")),
        name="gla",
    )(q, k, v, lf, s0, tri, g2)


def _t5_shifted_bias(n, tab_ref, h):
    nf = jnp.maximum(n, 1).astype(F32)
    large = T5_MAX_EXACT + (jnp.log(nf / T5_MAX_EXACT) / math.log(T5_MAX_DIST / T5_MAX_EXACT)
                            * (N_BUCKETS - T5_MAX_EXACT)).astype(jnp.int32)
    large = jnp.minimum(large, N_BUCKETS - 1)
    bucket = jnp.where(n < T5_MAX_EXACT, n, large)
    far = tab_ref[h * N_BUCKETS + N_BUCKETS - 1]
    val = jnp.zeros(n.shape, F32)
    for b in range(N_BUCKETS - 1):
        val = jnp.where(bucket == b, (tab_ref[h * N_BUCKETS + b] - far) * LOG2E, val)
    return val


def _bias_prompt_body(tab_ref, o_ref, *, T, RB):
    h = pl.program_id(0)
    t = pl.program_id(1)
    for r0 in range(0, T, RB):
        blocks = [c0 for c0 in range(0, T, LANES)
                  if (r0 - (c0 + LANES - 1) < T5_MAX_DIST and r0 + RB - 1 - c0 >= 0)
                  or T + r0 - (c0 + LANES - 1) < T5_MAX_DIST]
        for c0 in range(0, T, LANES):
            r = lax.broadcasted_iota(jnp.int32, (RB, LANES), 0) + r0
            c = lax.broadcasted_iota(jnp.int32, (RB, LANES), 1) + c0
            n = r - c + t * T
            if c0 in blocks:
                val = _t5_shifted_bias(jnp.maximum(n, 0), tab_ref, h)
            else:
                val = jnp.zeros((RB, LANES), F32)
            o_ref[0, 0, r0:r0 + RB, c0:c0 + LANES] = jnp.where(n >= 0, val, NEG)


def _bias_prompt_tiles(tab, *, T):
    assert T >= T5_MAX_DIST
    return pl.pallas_call(
        functools.partial(_bias_prompt_body, T=T, RB=64),
        grid=(H_D, 2),
        in_specs=[pl.BlockSpec(memory_space=pltpu.SMEM)],
        out_specs=pl.BlockSpec((1, 1, T, T), lambda h, t: (h, t, 0, 0)),
        out_shape=jax.ShapeDtypeStruct((H_D, 2, T, T), F32),
        compiler_params=_cparams(("parallel", "parallel")),
        name="bias_prompt",
    )(tab)


def _bias_paged_body(tab_ref, o_ref, *, LQ):
    t = pl.program_id(0)
    rh = 2 * LQ
    r = lax.broadcasted_iota(jnp.int32, (rh, PAGE), 0)
    c = lax.broadcasted_iota(jnp.int32, (rh, PAGE), 1)
    n = (r - _idiv(r, LQ) * LQ) - c + (1 - t) * PAGE
    for h in range(H_D):
        val = _t5_shifted_bias(jnp.maximum(n, 0), tab_ref, h)
        o_ref[0, h * rh:(h + 1) * rh, :] = jnp.where(n >= 0, val, NEG)


def _bias_paged_tiles(tab, *, LQ):
    assert PAGE >= T5_MAX_DIST
    R = H_D * 2 * LQ
    return pl.pallas_call(
        functools.partial(_bias_paged_body, LQ=LQ),
        grid=(2,),
        in_specs=[pl.BlockSpec(memory_space=pltpu.SMEM)],
        out_specs=pl.BlockSpec((1, R, PAGE), lambda t: (t, 0, 0)),
        out_shape=jax.ShapeDtypeStruct((2, R, PAGE), F32),
        compiler_params=_cparams(("parallel",)),
        name="bias_paged",
    )(tab)


def _dattn_body(q_ref, k_ref, v_ref, bias_ref, o_ref, qm_scr, m_scr, l_scr, acc_scr, *, T):
    qi = pl.program_id(2)
    m_scr[...] = jnp.full_like(m_scr, NEG)
    l_scr[...] = jnp.zeros_like(l_scr)
    acc_scr[...] = jnp.zeros_like(acc_scr)
    q = q_ref[0]
    lane = lax.broadcasted_iota(jnp.int32, q.shape, 1)
    qm_scr[0] = jnp.where(lane < DK_D, q, jnp.zeros_like(q))
    qm_scr[1] = jnp.where(lane >= DK_D, q, jnp.zeros_like(q))

    def step(kj, r0, nr, bias):
        c0 = pl.multiple_of(kj * T, T)
        k = k_ref[0, pl.ds(c0, T), :]
        v = v_ref[0, pl.ds(c0, T), :]
        for mp in range(2):
            s = _nt_dot(qm_scr[mp, r0:r0 + nr, :], k)
            if bias is not None:
                s = s + bias
            m_prev = m_scr[mp, r0:r0 + nr, :]
            m_new = jnp.maximum(m_prev, jnp.max(s, axis=-1, keepdims=True))
            corr = jnp.exp2(m_prev - m_new)
            p = jnp.exp2(s - jnp.concatenate([m_new] * (T // LANES), axis=1))
            l_scr[mp, r0:r0 + nr, :] = l_scr[mp, r0:r0 + nr, :] * corr + jnp.sum(p, axis=-1, keepdims=True)
            acc_scr[mp, r0:r0 + nr, :] = (acc_scr[mp, r0:r0 + nr, :] * corr
                                          + jnp.dot(p.astype(BF16), v, preferred_element_type=F32))
            m_scr[mp, r0:r0 + nr, :] = m_new

    def far_step(kj, carry):
        step(kj, 0, 2 * T, None)
        return carry

    lax.fori_loop(0, jnp.maximum(2 * qi - 1, 0), far_step, 0)

    diag = bias_ref[0, 0]
    sub = bias_ref[0, 1]

    @pl.when(qi >= 1)
    def _():
        step(2 * qi - 1, 0, 2 * T, jnp.concatenate([sub, jnp.zeros_like(sub)], axis=0))

    step(2 * qi, 0, 2 * T, jnp.concatenate([diag, sub], axis=0))
    step(2 * qi + 1, T, T, diag)
    for mp in range(2):
        o_ref[0, :, mp * DV_D:(mp + 1) * DV_D] = acc_scr[mp] / l_scr[mp]


def _dattn_prompt(q, k, v, bias_tiles, *, T):
    bsz, seq, _ = q.shape
    tq = 2 * T
    kv_spec = pl.BlockSpec((1, seq, 128), lambda b, h, qi: (b, 0, h))
    return pl.pallas_call(
        functools.partial(_dattn_body, T=T),
        grid=(bsz, H_D, seq // tq),
        in_specs=[pl.BlockSpec((1, tq, 128), lambda b, h, qi: (b, qi, h)), kv_spec, kv_spec,
                  pl.BlockSpec((1, 2, T, T), lambda b, h, qi: (h, 0, 0, 0))],
        out_specs=pl.BlockSpec((1, tq, 2 * DV_D), lambda b, h, qi: (b, qi, h)),
        out_shape=jax.ShapeDtypeStruct((bsz, seq, H_D * 2 * DV_D), F32),
        scratch_shapes=[pltpu.VMEM((2, tq, 128), BF16), pltpu.VMEM((2, tq, LANES), F32),
                        pltpu.VMEM((2, tq, LANES), F32), pltpu.VMEM((2, tq, DV_D), F32)],
        compiler_params=_cparams(("parallel", "parallel", "arbitrary")),
        name="diff_attn_prompt",
    )(q, k, v, bias_tiles)


def _paged_phases(j, n_j, q_ref, kn_ref, vn_ref, bias_ref, k_refs, v_refs, o_ref,
                  wq_scr, m_scr, l_scr, acc_scr, *, LQ):
    P = len(k_refs)
    RH = 2 * LQ
    is_last = j == n_j - 1

    def update(s, v_heads):
        n = s.shape[1] // PAGE
        m_prev = m_scr[...]
        m_new = jnp.maximum(m_prev, jnp.max(s, axis=-1, keepdims=True))
        corr = jnp.exp2(m_prev - m_new)
        p = jnp.exp2(s - jnp.concatenate([m_new] * n, axis=1))
        l_scr[...] = l_scr[...] * corr + jnp.sum(p, axis=-1, keepdims=True)
        pb = p.astype(BF16)
        pv = [jnp.dot(pb[h * RH:(h + 1) * RH, :], v_heads[h], preferred_element_type=F32)
              for h in range(H_D)]
        acc_scr[...] = acc_scr[...] * corr + jnp.concatenate(pv, axis=0)
        m_scr[...] = m_new

    def init():
        @pl.when(j == 0)
        def _():
            q = q_ref[0]
            qt = jnp.concatenate([q] * (H_D * 2), axis=0)
            rr = lax.broadcasted_iota(jnp.int32, qt.shape, 0)
            cc = lax.broadcasted_iota(jnp.int32, qt.shape, 1)
            wq_scr[...] = jnp.where(_idiv(rr, LQ) == _idiv(cc, DK_D), qt, jnp.zeros_like(qt))
            m_scr[...] = jnp.full_like(m_scr, NEG)
            l_scr[...] = jnp.zeros_like(l_scr)
            acc_scr[...] = jnp.zeros_like(acc_scr)
            pad = jnp.zeros((PAGE - LQ, H_D * 2 * DK_D), BF16)
            kn = jnp.concatenate([kn_ref[0], pad], axis=0)
            vn = jnp.concatenate([vn_ref[0], pad], axis=0)
            update(_nt_dot(wq_scr[...], kn) + bias_ref[1],
                   [vn[:, h * DV_D:(h + 1) * DV_D] for h in range(H_D)])

    def main():
        kt = jnp.concatenate([k_refs[i][0].astype(BF16) for i in range(P)], axis=1)
        s = jnp.dot(wq_scr[...], kt, preferred_element_type=F32)
        last_bias = bias_ref[0] * jnp.where(is_last, 1.0, 0.0)
        s = jnp.concatenate([s[:, :(P - 1) * PAGE], s[:, (P - 1) * PAGE:] + last_bias], axis=1)
        v_heads = [jnp.concatenate([v_refs[i][0, pl.ds(h, PAGE, stride=H_D), :].astype(BF16)
                                    for i in range(P)], axis=0) for h in range(H_D)]
        update(s, v_heads)

    def final():
        @pl.when(is_last)
        def _():
            o_ref[0] = acc_scr[...] / l_scr[...]

    return init, main, final


def _paged_gla_body(pt_ref, *refs, P, LQ, n_j, n_tb, C, n_chunks):
    q_ref, kn_ref, vn_ref, bias_ref = refs[:4]
    k_refs = refs[4:4 + P]
    v_refs = refs[4 + P:4 + 2 * P]
    gla_in = refs[4 + 2 * P:4 + 2 * P + 7]
    o_ref, go_ref, gs_ref = refs[4 + 2 * P + 7:4 + 2 * P + 10]
    paged_scr = refs[4 + 2 * P + 10:4 + 2 * P + 14]
    gla_scr = refs[4 + 2 * P + 14:]
    s = pl.program_id(0)
    gla = _gla_phases(lax.rem(s, n_tb), n_tb, *gla_in, go_ref, gs_ref, *gla_scr, C=C, n_chunks=n_chunks,
                      unroll=n_chunks)
    paged = _paged_phases(lax.rem(s, n_j), n_j, q_ref, kn_ref, vn_ref, bias_ref, k_refs, v_refs, o_ref,
                          *paged_scr, LQ=LQ)
    for g_phase, p_phase in zip(gla, paged):
        g_phase()
        p_phase()


def _paged_attn_with_gla(q, k_new, v_new, kt_pool, v_pool, page_table, bias_tiles,
                         gq, gk, gv, glf, gs0, *, P):
    bd, lq, _ = q.shape
    n_pages = page_table.shape[1]
    n_j = n_pages // P
    steps = bd * n_j
    bsz, seq, _ = gq.shape
    C = math.gcd(seq, GLA_CHUNK)
    tb = bsz * seq // steps
    assert tb * steps == bsz * seq and tb % C == 0 and seq % tb == 0
    n_tb = seq // tb
    tri, g2 = _gla_consts(tb, C)
    R = H_D * 2 * lq
    tok = pl.BlockSpec((1, lq, 512), lambda s, pt: (s // n_j, 0, 0))
    const = lambda shape: pl.BlockSpec(shape, lambda s, pt: (0,) * len(shape))

    def page_spec(i):
        return pl.BlockSpec((1, 512, PAGE), lambda s, pt: (pt[(s // n_j) * n_pages + (s % n_j) * P + i], 0, 0))

    gtok = lambda w: pl.BlockSpec((1, tb, w), lambda s, pt: (s // n_tb, s % n_tb, 0))
    gst = pl.BlockSpec((1, H_G, DK_G, DV_G), lambda s, pt: (s // n_tb, 0, 0, 0))
    grid_spec = pltpu.PrefetchScalarGridSpec(
        num_scalar_prefetch=1,
        grid=(steps,),
        in_specs=[tok, tok, tok, const((2, R, PAGE))] + [page_spec(i) for i in range(P)] * 2
        + [gtok(256), gtok(256), gtok(512), gtok(256), gst, const((tb, tb)), const((256, 512))],
        out_specs=[pl.BlockSpec((1, R, DV_D), lambda s, pt: (s // n_j, 0, 0)), gtok(512), gst],
        scratch_shapes=[pltpu.VMEM((R, 512), BF16), pltpu.VMEM((R, LANES), F32),
                        pltpu.VMEM((R, LANES), F32), pltpu.VMEM((R, DV_D), F32)] + _gla_scratch(tb),
    )
    return pl.pallas_call(
        functools.partial(_paged_gla_body, P=P, LQ=lq, n_j=n_j, n_tb=n_tb, C=C, n_chunks=tb // C),
        grid_spec=grid_spec,
        out_shape=[jax.ShapeDtypeStruct((bd, R, DV_D), F32),
                   jax.ShapeDtypeStruct((bsz, seq, 512), F32),
                   jax.ShapeDtypeStruct((bsz, H_G, DK_G, DV_G), F32)],
        compiler_params=_cparams(("arbitrary",)),
        name="paged_attn_gla",
    )(page_table.reshape(-1), q, k_new, v_new, bias_tiles, *([kt_pool] * P), *([v_pool] * P),
      gq, gk, gv, glf, gs0, tri, g2)


def _mixout_math(x, og, rg, od, lam4, gn, sn, wo, lam_init):
    lam = (jnp.exp(jnp.sum(lam4[0:1, :] * lam4[1:2, :], axis=-1, keepdims=True))
           - jnp.exp(jnp.sum(lam4[2:3, :] * lam4[3:4, :], axis=-1, keepdims=True)) + lam_init)
    parts = []
    for h in range(H_G):
        t = og[:, h * DV_G:(h + 1) * DV_G]
        r = rg[:, h * DV_G:(h + 1) * DV_G]
        parts.append(_rms_full(t, gn) * (r * jax.nn.sigmoid(r)))
    for h in range(H_D):
        d = od[:, h * 2 * DV_D:h * 2 * DV_D + DV_D] - lam * od[:, h * 2 * DV_D + DV_D:(h + 1) * 2 * DV_D]
        parts.append(_rms_full(d, sn) * (1.0 - lam_init))
    cat = jnp.concatenate(parts, axis=-1).astype(BF16)
    return x + jnp.dot(cat, wo, preferred_element_type=F32)


def _mixout_body(x_ref, og_ref, rg_ref, od_ref, lam4_ref, gn_ref, sn_ref, wo_ref, o_ref, *, lam_init):
    o_ref[...] = _mixout_math(x_ref[...], og_ref[...], rg_ref[...], od_ref[...], lam4_ref[...],
                              gn_ref[...], sn_ref[...], wo_ref[...], lam_init)


def _mix_xattn_body(x_ref, og_ref, rg_ref, od_ref, lam4_ref, gn_ref, sn_ref, wo_ref,
                    xg_ref, wq_ref, qn_ref, mk_ref, mv_ref, wox_ref, o_ref, *, lam_init, dh):
    x = _mixout_math(x_ref[...], og_ref[...], rg_ref[...], od_ref[...], lam4_ref[...],
                     gn_ref[...], sn_ref[...], wo_ref[...], lam_init)
    h = _rms_full(x, xg_ref[...]).astype(BF16)
    y = jnp.dot(h, wq_ref[...], preferred_element_type=F32)
    qn = qn_ref[...]
    k = mk_ref[0]
    v = mv_ref[0]
    outs = []
    for i in range(H_X):
        sl = slice(i * dh, (i + 1) * dh)
        q = (_rms_full(y[:, sl], qn) * (dh ** -0.5)).astype(BF16)
        s = _nt_dot(q, k[:, sl])
        p = jnp.exp(s - jnp.max(s, axis=-1, keepdims=True))
        p = p / jnp.sum(p, axis=-1, keepdims=True)
        outs.append(jnp.dot(p.astype(BF16), v[:, sl], preferred_element_type=F32))
    att = jnp.concatenate(outs, axis=-1).astype(BF16)
    o_ref[...] = x + jnp.dot(att, wox_ref[...], preferred_element_type=F32)


def _mix_xattn(x, og, rg, od, lam4, gn, sn, wo, xg, wq, qn, mk, mv, wox, *, tm, seq, lam_init):
    n, d = x.shape
    nt = seq // tm
    resident = lambda shape: pl.BlockSpec(shape, lambda i: (0,) * len(shape), pipeline_mode=pl.Buffered(1))
    row = lambda w: pl.BlockSpec((tm, w), lambda i: (i, 0))
    mem = pl.BlockSpec((1,) + mk.shape[1:], lambda i: (i // nt, 0, 0))
    return pl.pallas_call(
        functools.partial(_mix_xattn_body, lam_init=lam_init, dh=d // H_X),
        grid=(n // tm,),
        in_specs=[row(d), row(512), row(512), row(1024), resident((4, DK_D)), resident((1, DV_G)),
                  resident((1, DV_D)), resident(wo.shape), resident((1, d)), resident(wq.shape),
                  resident(qn.shape), mem, mem, resident(wox.shape)],
        out_specs=row(d),
        out_shape=jax.ShapeDtypeStruct((n, d), F32),
        compiler_params=_cparams(("parallel",)),
        name="mixer_out_xattn",
    )(x, og, rg, od, lam4, gn, sn, wo, xg, wq, qn, mk, mv, wox)


def _mixout(x, og, rg, od, lam4, gn, sn, wo, *, tm, lam_init):
    n, d = x.shape
    full = lambda shape: pl.BlockSpec(shape, lambda i: (0, 0))
    row = lambda w: pl.BlockSpec((tm, w), lambda i: (i, 0))
    return pl.pallas_call(
        functools.partial(_mixout_body, lam_init=lam_init),
        grid=(n // tm,),
        in_specs=[row(d), row(512), row(512), row(1024), full((4, DK_D)), full((1, DV_G)),
                  full((1, DV_D)), full(wo.shape)],
        out_specs=row(d),
        out_shape=jax.ShapeDtypeStruct((n, d), F32),
        compiler_params=_cparams(("parallel",)),
        name="mixer_out",
    )(x, og, rg, od, lam4, gn, sn, wo)


def _proj_body(x_ref, g_ref, w_ref, hn_ref, *o_refs, head_norm, scale, dh):
    h = _rms_full(x_ref[...], g_ref[...]).astype(BF16)
    y = jnp.dot(h, w_ref[...], preferred_element_type=F32)
    if head_norm:
        hn = hn_ref[...]
        y = jnp.concatenate(
            [_rms_full(y[:, i * dh:(i + 1) * dh], hn) for i in range(y.shape[1] // dh)], axis=-1)
    y = y * scale
    for o_ref in o_refs:
        o_ref[...] = y.astype(o_ref.dtype)


def _proj(x, g, w, hn, *, tm, head_norm, scale, out_dtypes):
    n, d = x.shape
    dout = w.shape[1]
    dh = hn.shape[1]
    full = lambda shape: pl.BlockSpec(shape, lambda i: (0, 0))
    row = lambda wd: pl.BlockSpec((tm, wd), lambda i: (i, 0))
    return pl.pallas_call(
        functools.partial(_proj_body, head_norm=head_norm, scale=scale, dh=dh),
        grid=(n // tm,),
        in_specs=[row(d), full((1, d)), full(w.shape), full((1, dh))],
        out_specs=[row(dout) for _ in out_dtypes],
        out_shape=[jax.ShapeDtypeStruct((n, dout), dt) for dt in out_dtypes],
        compiler_params=_cparams(("parallel",)),
        name="proj",
    )(x, g, w, hn)


def _xattn_cached_body(q_ref, k_ref, v_ref, o_ref, *, dh, M, G):
    nlb = dh // LANES
    stride = nlb * H_X
    lq = q_ref.shape[1]
    s_rows = []
    for g in range(G):
        q = q_ref[g]
        for h in range(H_X):
            s = None
            for lb in range(nlb):
                kh = k_ref[g, pl.ds(lb * H_X + h, M, stride=stride), :].astype(BF16)
                t = _nt_dot(q[:, h * dh + lb * LANES:h * dh + (lb + 1) * LANES], kh)
                s = t if s is None else s + t
            s_rows.append(s)
    s = jnp.concatenate(s_rows, axis=0)
    m = jnp.max(s, axis=-1, keepdims=True)
    p = jnp.exp(s - m)
    p = p / jnp.sum(p, axis=-1, keepdims=True)
    for g in range(G):
        outs = []
        for h in range(H_X):
            r0 = (g * H_X + h) * lq
            ph = p[r0:r0 + lq, :].astype(BF16)
            for lb in range(nlb):
                vh = v_ref[g, pl.ds(lb * H_X + h, M, stride=stride), :].astype(BF16)
                outs.append(jnp.dot(ph, vh, preferred_element_type=F32))
        o_ref[g] = jnp.concatenate(outs, axis=-1).astype(o_ref.dtype)


def _xattn_cached(q, k_rows, v_rows, *, M, G):
    bsz, lq, d = q.shape
    kv = pl.BlockSpec((G, k_rows.shape[1], LANES), lambda b: (b, 0, 0))
    return pl.pallas_call(
        functools.partial(_xattn_cached_body, dh=d // H_X, M=M, G=G),
        grid=(bsz // G,),
        in_specs=[pl.BlockSpec((G, lq, d), lambda b: (b, 0, 0)), kv, kv],
        out_specs=pl.BlockSpec((G, lq, d), lambda b: (b, 0, 0)),
        out_shape=jax.ShapeDtypeStruct((bsz, lq, d), BF16),
        compiler_params=_cparams(("parallel",)),
        name="xattn_cached",
    )(q, k_rows, v_rows)


def _resproj_body(x_ref, a_ref, w_ref, o_ref):
    o_ref[...] = x_ref[...] + jnp.dot(a_ref[...], w_ref[...], preferred_element_type=F32)


def _resproj(x, a, w, *, tm):
    n, d = x.shape
    row = lambda wd: pl.BlockSpec((tm, wd), lambda i: (i, 0))
    return pl.pallas_call(
        _resproj_body,
        grid=(n // tm,),
        in_specs=[row(d), row(a.shape[1]), pl.BlockSpec(w.shape, lambda i: (0, 0))],
        out_specs=row(d),
        out_shape=jax.ShapeDtypeStruct((n, d), F32),
        compiler_params=_cparams(("parallel",)),
        name="resproj",
    )(x, a, w)


def kernel(x_prompt, x_sample, mem_prompt, cache_diff_k, cache_diff_v, state_gla, cache_mem_k, cache_mem_v, page_table, rel_bias, ffn1_norm, ffn1_w_gate, ffn1_w_up, ffn1_w_down, mix_norm, w_in, gla_w_gate_up, gla_b_gate, gla_out_norm, diff_q_norm, diff_k_norm, diff_lam_q1, diff_lam_k1, diff_lam_q2, diff_lam_k2, diff_subln, w_out, xattn_norm, mem_norm, xattn_wq, xattn_wk, xattn_wv, xattn_q_norm, xattn_k_norm, xattn_wo, ffn2_norm, ffn2_w_gate, ffn2_w_up, ffn2_w_down):
    depth = ffn1_norm.shape[0]
    B, L, D = x_prompt.shape
    Bd, Ld, _ = x_sample.shape
    M = mem_prompt.shape[1]
    n_pool = cache_diff_k.shape[1]
    T = 512
    TM = 512
    P_PAGES = 32

    xp = x_prompt.reshape(B * L, D)
    xs = x_sample.reshape(Bd * Ld, D)
    outs = {k: [] for k in ("kp", "vp", "ks", "vs", "sp", "ss", "mk", "mv")}

    g64 = jnp.asarray(np.kron(np.eye(512 // DK_D, dtype=np.float32), np.ones((DK_D, DK_D), np.float32)), BF16)

    for l in range(depth):
        lam_init = 0.8 - 0.6 * math.exp(-0.3 * l)
        row = lambda a: a[l].reshape(1, -1)

        w1g, w1u, w1d = (w[l].astype(BF16) for w in (ffn1_w_gate, ffn1_w_up, ffn1_w_down))
        w2g, w2u, w2d = (w[l].astype(BF16) for w in (ffn2_w_gate, ffn2_w_up, ffn2_w_down))
        wi = w_in[l]
        lr0 = 2 * H_G * DK_G + 2 * H_G * DV_G
        w_main = jnp.concatenate([wi[:, :lr0], wi[:, lr0 + GLA_RANK:]], axis=1).astype(BF16)
        w_lr = jnp.pad(wi[:, lr0:lr0 + GLA_RANK], ((0, 0), (0, 128 - GLA_RANK))).astype(BF16)
        w_up = jnp.pad(gla_w_gate_up[l], ((0, 128 - GLA_RANK), (0, 0))).astype(BF16)
        qn = jnp.tile(diff_q_norm[l], 2 * H_D).reshape(1, -1)
        kn = jnp.tile(diff_k_norm[l], 2 * H_D).reshape(1, -1)
        lam4 = jnp.stack([diff_lam_q1[l], diff_lam_k1[l], diff_lam_q2[l], diff_lam_k2[l]]).astype(F32)
        wo_mix = w_out[l].astype(BF16)
        wq_x, wk_x, wv_x, wo_x = (w[l].astype(BF16) for w in (xattn_wq, xattn_wk, xattn_wv, xattn_wo))

        tab = rel_bias.astype(F32).T.reshape(-1)
        bias_tiles = _bias_prompt_tiles(tab, T=T)
        bias_paged = _bias_paged_tiles(tab, LQ=Ld)

        xp = _ffn(xp, row(ffn1_norm), w1g, w1u, w1d, tm=1024, tf=256)
        xs = _ffn(xs, row(ffn1_norm), w1g, w1u, w1d, tm=1024, tf=256)

        mix_args = (row(mix_norm), w_main, w_lr, w_up, row(gla_b_gate), qn, kn, g64)
        qg, kg, vg, rg, lf, qd, kdt, kdb, vd, vdb = _mixin(xp, *mix_args, tm=TM, seq=L)
        qg_s, kg_s, vg_s, rg_s, lf_s, qd_s, kd_s, kdb_s, vd_s, vdb_s = _mixin(xs, *mix_args, tm=TM)
        b3 = lambda a: a.reshape(B, L, -1)
        s3 = lambda a: a.reshape(Bd, Ld, -1)
        og_s, ss = _gla(s3(qg_s), s3(kg_s), s3(vg_s), s3(lf_s), state_gla[l], tb=Ld, G=4)

        kt_pool = jnp.transpose(cache_diff_k[l], (0, 2, 3, 4, 1)).reshape(n_pool, H_D * 2 * DK_D, PAGE)
        v_pool = cache_diff_v[l].reshape(n_pool, PAGE * H_D, DV_D)
        s0 = jnp.zeros((B, H_G, DK_G, DV_G), F32)
        od_s, og, sp = _paged_attn_with_gla(s3(qd_s), s3(kdb_s), s3(vdb_s), kt_pool, v_pool, page_table,
                                            bias_paged, b3(qg), b3(kg), b3(vg), b3(lf), s0, P=P_PAGES)

        od = _dattn_prompt(b3(qd), b3(kdb), b3(vdb), bias_tiles, T=T)
        dh = D // H_X
        mem2 = mem_prompt.reshape(B * M, D)
        mk, mkb = _proj(mem2, row(mem_norm), wk_x, row(xattn_k_norm), tm=TM, head_norm=True, scale=1.0,
                        out_dtypes=(F32, BF16))
        mv, mvb = _proj(mem2, row(mem_norm), wv_x, jnp.ones((1, dh), F32), tm=TM, head_norm=False,
                        scale=1.0, out_dtypes=(F32, BF16))
        xp = _mix_xattn(xp, og.reshape(B * L, -1), rg, od.reshape(B * L, -1), lam4, row(gla_out_norm),
                        row(diff_subln), wo_mix, row(xattn_norm), wq_x, row(xattn_q_norm),
                        mkb.reshape(B, M, D), mvb.reshape(B, M, D), wo_x, tm=TM, seq=L, lam_init=lam_init)
        outs["mk"].append(mk.reshape(B, M, H_X, dh))
        outs["mv"].append(mv.reshape(B, M, H_X, dh))
        outs["kp"].append(kdt.reshape(B, H_D, 2, DK_D, L).transpose(0, 4, 1, 2, 3))
        outs["vp"].append(vd.reshape(B, L, H_D, DV_D))
        outs["sp"].append(sp)

        od_s = od_s.reshape(Bd, H_D, 2, Ld, DV_D).transpose(0, 3, 1, 2, 4).reshape(Bd * Ld, -1)
        xs = _mixout(xs, og_s.reshape(Bd * Ld, -1), rg_s, od_s, lam4, row(gla_out_norm), row(diff_subln),
                     wo_mix, tm=TM, lam_init=lam_init)
        outs["ks"].append(kd_s.reshape(Bd, Ld, H_D, 2, DK_D))
        outs["vs"].append(vd_s.reshape(Bd, Ld, H_D, DV_D))
        outs["ss"].append(ss)

        nlb = dh // LANES
        (qs,) = _proj(xs, row(xattn_norm), wq_x, row(xattn_q_norm), tm=TM, head_norm=True,
                      scale=dh ** -0.5, out_dtypes=(BF16,))
        cache_rows = lambda c: (c[l].reshape(Bd, M, H_X, nlb, LANES).transpose(0, 1, 3, 2, 4)
                                .reshape(Bd, M * nlb * H_X, LANES))
        osm = _xattn_cached(qs.reshape(Bd, Ld, D), cache_rows(cache_mem_k), cache_rows(cache_mem_v), M=M,
                            G=4)
        xs = _resproj(xs, osm.reshape(Bd * Ld, D), wo_x, tm=TM)

        xp = _ffn(xp, row(ffn2_norm), w2g, w2u, w2d, tm=1024, tf=256)
        xs = _ffn(xs, row(ffn2_norm), w2g, w2u, w2d, tm=1024, tf=256)

    st = lambda key: jnp.stack(outs[key])
    return (xp.reshape(B, L, D), xs.reshape(Bd, Ld, D), st("kp"), st("vp"), st("ks"), st("vs"),
            st("sp"), st("ss"), st("mk"), st("mv"))
```

```python
import functools
import math

import numpy as np
import jax
import jax.numpy as jnp
from jax import lax
from jax.experimental import pallas as pl
from jax.experimental.pallas import tpu as pltpu

F32 = jnp.float32
BF16 = jnp.bfloat16

EPS = 1e-6
NEG = -1e30
LOG2E = 1.0 / math.log(2.0)

H_G, DK_G, DV_G = 4, 64, 128
GLA_RANK = 16
GLA_GATE_NORM = 16.0
GLA_CHUNK = 16
H_D, DK_D, DV_D = 4, 64, 128
N_BUCKETS = 32
T5_MAX_EXACT = N_BUCKETS // 2
T5_MAX_DIST = 128
H_X = 4
PAGE = 128
LANES = 128

VMEM_LIMIT = 56 * 1024 * 1024


def _cparams(sem):
    return pltpu.CompilerParams(dimension_semantics=sem, vmem_limit_bytes=VMEM_LIMIT)


def _rms_full(x, g):
    ms = jnp.mean(x * x, axis=-1, keepdims=True)
    return x * lax.rsqrt(ms + EPS) * g


def _idiv(x, d):
    if d & (d - 1) == 0:
        return lax.shift_right_logical(x, int(math.log2(d)))
    return x // d


def _nt_dot(a, b):
    return lax.dot_general(a, b, (((1,), (1,)), ((), ())), preferred_element_type=F32)


def _ffn_body(x_ref, g_ref, wg_ref, wu_ref, wd_ref, o_ref, h_scr, acc_scr, *, tf):
    h_scr[...] = _rms_full(x_ref[...], g_ref[...]).astype(BF16)
    acc_scr[...] = jnp.zeros_like(acc_scr)

    def chunk(j, carry):
        c0 = pl.multiple_of(j * tf, tf)
        h = h_scr[...]
        gate = jnp.dot(h, wg_ref[:, pl.ds(c0, tf)], preferred_element_type=F32)
        up = jnp.dot(h, wu_ref[:, pl.ds(c0, tf)], preferred_element_type=F32)
        a = (gate * jax.nn.sigmoid(gate) * up).astype(BF16)
        acc_scr[...] += jnp.dot(a, wd_ref[pl.ds(c0, tf), :], preferred_element_type=F32)
        return carry

    lax.fori_loop(0, wg_ref.shape[1] // tf, chunk, 0)
    o_ref[...] = x_ref[...] + 0.5 * acc_scr[...]


def _ffn(x, g, wg, wu, wd, *, tm, tf):
    n, d = x.shape
    resident = lambda shape: pl.BlockSpec(shape, lambda i: (0, 0), pipeline_mode=pl.Buffered(1))
    return pl.pallas_call(
        functools.partial(_ffn_body, tf=tf),
        grid=(n // tm,),
        in_specs=[pl.BlockSpec((tm, d), lambda i: (i, 0)), resident((1, d)),
                  resident(wg.shape), resident(wu.shape), resident(wd.shape)],
        out_specs=pl.BlockSpec((tm, d), lambda i: (i, 0)),
        out_shape=jax.ShapeDtypeStruct((n, d), F32),
        scratch_shapes=[pltpu.VMEM((tm, d), BF16), pltpu.VMEM((tm, d), F32)],
        compiler_params=_cparams(("parallel",)),
        name="ffn",
    )(x, g, wg, wu, wd)


def _mixin_body(x_ref, g_ref, w_ref, wlr_ref, wup_ref, bg_ref, qn_ref, kn_ref, g64_ref,
                qg_ref, kg_ref, vg_ref, rg_ref, lf_ref, qd_ref, kd_ref, kdb_ref, vd_ref, vdb_ref,
                *, k_transposed):
    h = _rms_full(x_ref[...], g_ref[...]).astype(BF16)
    y = jnp.dot(h, w_ref[...], preferred_element_type=F32)
    qg_ref[...] = y[:, 0:256] * (DK_G ** -0.5)
    kg_ref[...] = y[:, 256:512]
    vg_ref[...] = y[:, 512:1024]
    rg_ref[...] = y[:, 1024:1536]
    qd = y[:, 1536:2048]
    kd = y[:, 2048:2560]
    vd = y[:, 2560:3072]

    glr = jnp.dot(h, wlr_ref[...], preferred_element_type=F32)
    z = jnp.dot(glr.astype(BF16), wup_ref[...], preferred_element_type=F32) + bg_ref[...]
    lf_ref[...] = (jnp.minimum(z, 0.0) - jnp.log(1.0 + jnp.exp(-jnp.abs(z)))) * (1.0 / GLA_GATE_NORM)

    def group_rms(t, gain):
        msq = jnp.dot((t * t).astype(BF16), g64_ref[...], preferred_element_type=F32) * (1.0 / DK_D)
        return t * lax.rsqrt(msq + EPS) * gain

    qdn = group_rms(qd, qn_ref[...]) * (DK_D ** -0.5 * LOG2E)
    kdn = group_rms(kd, kn_ref[...])
    qd_ref[...] = qdn.astype(BF16)
    if k_transposed:
        kd_ref[0] = kdn.T
    else:
        kd_ref[...] = kdn
    kdb_ref[...] = kdn.astype(BF16)
    for h in range(H_D):
        vd_ref[pl.ds(h, vd.shape[0], stride=H_D), :] = vd[:, h * DV_D:(h + 1) * DV_D]
    vdb_ref[...] = vd.astype(BF16)


def _mixin(x, g, w_main, w_lr, w_up, b_gate, qn, kn, g64, *, tm, seq=None):
    n, d = x.shape
    full = lambda shape: pl.BlockSpec(shape, lambda i: (0, 0))
    row = lambda w: pl.BlockSpec((tm, w), lambda i: (i, 0))
    outs = [(256, F32), (256, F32), (512, F32), (512, F32), (256, F32),
            (512, BF16), (512, F32), (512, BF16), (512, F32), (512, BF16)]
    out_specs = [row(w) for w, _ in outs]
    out_shape = [jax.ShapeDtypeStruct((n, w), dt) for w, dt in outs]
    if seq is not None:
        nt = seq // tm
        out_specs[6] = pl.BlockSpec((1, 512, tm), lambda i: (i // nt, 0, i % nt))
        out_shape[6] = jax.ShapeDtypeStruct((n // seq, 512, seq), F32)
    out_specs[8] = pl.BlockSpec((tm * H_D, DV_D), lambda i: (i, 0))
    out_shape[8] = jax.ShapeDtypeStruct((n * H_D, DV_D), F32)
    return pl.pallas_call(
        functools.partial(_mixin_body, k_transposed=seq is not None),
        grid=(n // tm,),
        in_specs=[row(d), full((1, d)), full(w_main.shape), full(w_lr.shape), full(w_up.shape),
                  full((1, 256)), full((1, 512)), full((1, 512)), full((512, 512))],
        out_specs=out_specs,
        out_shape=out_shape,
        compiler_params=_cparams(("parallel",)),
        name="mixer_in",
    )(x, g, w_main, w_lr, w_up, b_gate, qn, kn, g64)


def _gla_phases(tb, n_tb, q_ref, k_ref, v_ref, lf_ref, s0_ref, tri_ref, g2_ref, o_ref, so_ref,
                st_scr, nat_scr, b_scr, *, C, n_chunks, unroll):
    def init():
        @pl.when(tb == 0)
        def _():
            nat_scr[...] = jnp.zeros_like(nat_scr)
            for h in range(H_G):
                nat_scr[h * DK_G:(h + 1) * DK_G, h * DV_G:(h + 1) * DV_G] = s0_ref[0, h]
            st_scr[...] = nat_scr[...].T

    def final():
        @pl.when(tb == n_tb - 1)
        def _():
            nat = st_scr[...].T
            for h in range(H_G):
                so_ref[0, h] = nat[h * DK_G:(h + 1) * DK_G, h * DV_G:(h + 1) * DV_G]

    def chunk(c, carry):
        rows = lax.broadcasted_iota(jnp.int32, (H_G * DV_G, H_G * DK_G), 0)
        cols = lax.broadcasted_iota(jnp.int32, (H_G * DV_G, H_G * DK_G), 1)
        head_mask = _idiv(rows, DV_G) == _idiv(cols, DK_G)
        t_idx = lax.broadcasted_iota(jnp.int32, (C, H_G * DK_G), 0)
        r0 = pl.multiple_of(c * C, C)
        q = q_ref[0, pl.ds(r0, C), :]
        k = k_ref[0, pl.ds(r0, C), :]
        v = v_ref[0, pl.ds(r0, C), :]
        b = b_scr[pl.ds(r0, C), :]
        st = st_scr[...]

        o = _nt_dot((q * jnp.exp(b)).astype(BF16), st.astype(BF16))

        pieces = []
        for s in range(C):
            rel = jnp.where(t_idx >= s, b - b[s:s + 1, :], NEG)
            pieces.append(q * k[s:s + 1, :] * jnp.exp(rel))
        w = jnp.concatenate(pieces, axis=0).astype(BF16)
        a = jnp.dot(w, g2_ref[...], preferred_element_type=F32)
        for s in range(C):
            o = o + a[s * C:(s + 1) * C, :] * v[s:s + 1, :]
        o_ref[0, pl.ds(r0, C), :] = o

        b_last = b[C - 1:C, :]
        kt = (k * jnp.exp(b_last - b)).astype(BF16)
        upd = lax.dot_general(v.astype(BF16), kt, (((0,), (0,)), ((), ())),
                              preferred_element_type=F32)
        st_scr[...] = st * jnp.exp(b_last) + jnp.where(head_mask, upd, 0.0)
        return carry

    def main():
        b_scr[...] = jnp.dot(tri_ref[...], lf_ref[0], preferred_element_type=F32,
                             precision=lax.Precision.HIGHEST)
        lax.fori_loop(0, n_chunks, chunk, 0, unroll=unroll)

    return init, main, final


def _gla_body(*refs, C, n_chunks, G):
    q_ref, k_ref, v_ref, lf_ref, s0_ref, tri_ref, g2_ref, o_ref, so_ref = refs[:9]
    scratch = refs[9:]
    per_seq = []
    for g in range(G):
        one = lambda r, g=g: r.at[pl.ds(g, 1)]
        per_seq.append(_gla_phases(
            pl.program_id(1), pl.num_programs(1), one(q_ref), one(k_ref), one(v_ref), one(lf_ref),
            one(s0_ref), tri_ref, g2_ref, one(o_ref), one(so_ref), *[s.at[g] for s in scratch],
            C=C, n_chunks=n_chunks, unroll=min(2, n_chunks)))
    for phases in zip(*per_seq):
        for phase in phases:
            phase()


def _gla_consts(tb, C):
    tri = np.kron(np.eye(tb // C, dtype=np.float32), np.tril(np.ones((C, C), np.float32)))
    g2 = np.kron(np.eye(H_G, dtype=np.float32), np.ones((DK_G, DV_G), np.float32))
    return jnp.asarray(tri), jnp.asarray(g2, BF16)


def _gla_scratch(tb, lead=()):
    return [pltpu.VMEM(lead + (H_G * DV_G, H_G * DK_G), F32), pltpu.VMEM(lead + (H_G * DK_G, H_G * DV_G), F32),
            pltpu.VMEM(lead + (tb, H_G * DK_G), F32)]


def _gla(q, k, v, lf, s0, *, tb, G):
    bsz, seq, _ = q.shape
    C = math.gcd(seq, GLA_CHUNK)
    tb = min(tb, seq)
    tri, g2 = _gla_consts(tb, C)
    tok = lambda w: pl.BlockSpec((G, tb, w), lambda b, t: (b, t, 0))
    st_spec = pl.BlockSpec((G, H_G, DK_G, DV_G), lambda b, t: (b, 0, 0, 0))
    return pl.pallas_call(
        functools.partial(_gla_body, C=C, n_chunks=tb // C, G=G),
        grid=(bsz // G, seq // tb),
        in_specs=[tok(256), tok(256), tok(512), tok(256), st_spec,
                  pl.BlockSpec((tb, tb), lambda b, t: (0, 0)),
                  pl.BlockSpec((256, 512), lambda b, t: (0, 0))],
        out_specs=[tok(512), st_spec],
        out_shape=[jax.ShapeDtypeStruct((bsz, seq, 512), F32),
                   jax.ShapeDtypeStruct((bsz, H_G, DK_G, DV_G), F32)],
        scratch_shapes=_gla_scratch(tb, (G,)),
        compiler_params=_cparams(("parallel", "arbitrary")),
        name="gla",
    )(q, k, v, lf, s0, tri, g2)


def _t5_shifted_bias(n, tab_ref, h):
    nf = jnp.maximum(n, 1).astype(F32)
    large = T5_MAX_EXACT + (jnp.log(nf / T5_MAX_EXACT) / math.log(T5_MAX_DIST / T5_MAX_EXACT)
                            * (N_BUCKETS - T5_MAX_EXACT)).astype(jnp.int32)
    large = jnp.minimum(large, N_BUCKETS - 1)
    bucket = jnp.where(n < T5_MAX_EXACT, n, large)
    far = tab_ref[h * N_BUCKETS + N_BUCKETS - 1]
    val = jnp.zeros(n.shape, F32)
    for b in range(N_BUCKETS - 1):
        val = jnp.where(bucket == b, (tab_ref[h * N_BUCKETS + b] - far) * LOG2E, val)
    return val


def _bias_prompt_body(tab_ref, o_ref, *, T, RB):
    h = pl.program_id(0)
    t = pl.program_id(1)
    for r0 in range(0, T, RB):
        blocks = [c0 for c0 in range(0, T, LANES)
                  if (r0 - (c0 + LANES - 1) < T5_MAX_DIST and r0 + RB - 1 - c0 >= 0)
                  or T + r0 - (c0 + LANES - 1) < T5_MAX_DIST]
        for c0 in range(0, T, LANES):
            r = lax.broadcasted_iota(jnp.int32, (RB, LANES), 0) + r0
            c = lax.broadcasted_iota(jnp.int32, (RB, LANES), 1) + c0
            n = r - c + t * T
            if c0 in blocks:
                val = _t5_shifted_bias(jnp.maximum(n, 0), tab_ref, h)
            else:
                val = jnp.zeros((RB, LANES), F32)
            o_ref[0, 0, r0:r0 + RB, c0:c0 + LANES] = jnp.where(n >= 0, val, NEG)


def _bias_prompt_tiles(tab, *, T):
    assert T >= T5_MAX_DIST
    return pl.pallas_call(
        functools.partial(_bias_prompt_body, T=T, RB=64),
        grid=(H_D, 2),
        in_specs=[pl.BlockSpec(memory_space=pltpu.SMEM)],
        out_specs=pl.BlockSpec((1, 1, T, T), lambda h, t: (h, t, 0, 0)),
        out_shape=jax.ShapeDtypeStruct((H_D, 2, T, T), F32),
        compiler_params=_cparams(("parallel", "parallel")),
        name="bias_prompt",
    )(tab)


def _bias_paged_body(tab_ref, o_ref, *, LQ):
    t = pl.program_id(0)
    rh = 2 * LQ
    r = lax.broadcasted_iota(jnp.int32, (rh, PAGE), 0)
    c = lax.broadcasted_iota(jnp.int32, (rh, PAGE), 1)
    n = (r - _idiv(r, LQ) * LQ) - c + (1 - t) * PAGE
    for h in range(H_D):
        val = _t5_shifted_bias(jnp.maximum(n, 0), tab_ref, h)
        o_ref[0, h * rh:(h + 1) * rh, :] = jnp.where(n >= 0, val, NEG)


def _bias_paged_tiles(tab, *, LQ):
    assert PAGE >= T5_MAX_DIST
    R = H_D * 2 * LQ
    return pl.pallas_call(
        functools.partial(_bias_paged_body, LQ=LQ),
        grid=(2,),
        in_specs=[pl.BlockSpec(memory_space=pltpu.SMEM)],
        out_specs=pl.BlockSpec((1, R, PAGE), lambda t: (t, 0, 0)),
        out_shape=jax.ShapeDtypeStruct((2, R, PAGE), F32),
        compiler_params=_cparams(("parallel",)),
        name="bias_paged",
    )(tab)


def _dattn_body(q_ref, k_ref, v_ref, bias_ref, o_ref, qm_scr, m_scr, l_scr, acc_scr, *, T):
    qi = pl.program_id(2)
    m_scr[...] = jnp.full_like(m_scr, NEG)
    l_scr[...] = jnp.zeros_like(l_scr)
    acc_scr[...] = jnp.zeros_like(acc_scr)
    q = q_ref[0]
    lane = lax.broadcasted_iota(jnp.int32, q.shape, 1)
    qm_scr[0] = jnp.where(lane < DK_D, q, jnp.zeros_like(q))
    qm_scr[1] = jnp.where(lane >= DK_D, q, jnp.zeros_like(q))

    def step(kj, r0, nr, bias):
        c0 = pl.multiple_of(kj * T, T)
        k = k_ref[0, pl.ds(c0, T), :]
        v = v_ref[0, pl.ds(c0, T), :]
        for mp in range(2):
            s = _nt_dot(qm_scr[mp, r0:r0 + nr, :], k)
            if bias is not None:
                s = s + bias
            m_prev = m_scr[mp, r0:r0 + nr, :]
            m_new = jnp.maximum(m_prev, jnp.max(s, axis=-1, keepdims=True))
            corr = jnp.exp2(m_prev - m_new)
            p = jnp.exp2(s - jnp.concatenate([m_new] * (T // LANES), axis=1))
            l_scr[mp, r0:r0 + nr, :] = l_scr[mp, r0:r0 + nr, :] * corr + jnp.sum(p, axis=-1, keepdims=True)
            acc_scr[mp, r0:r0 + nr, :] = (acc_scr[mp, r0:r0 + nr, :] * corr
                                          + jnp.dot(p.astype(BF16), v, preferred_element_type=F32))
            m_scr[mp, r0:r0 + nr, :] = m_new

    def far_step(kj, carry):
        step(kj, 0, 2 * T, None)
        return carry

    lax.fori_loop(0, jnp.maximum(2 * qi - 1, 0), far_step, 0)

    diag = bias_ref[0, 0]
    sub = bias_ref[0, 1]

    @pl.when(qi >= 1)
    def _():
        step(2 * qi - 1, 0, 2 * T, jnp.concatenate([sub, jnp.zeros_like(sub)], axis=0))

    step(2 * qi, 0, 2 * T, jnp.concatenate([diag, sub], axis=0))
    step(2 * qi + 1, T, T, diag)
    for mp in range(2):
        o_ref[0, :, mp * DV_D:(mp + 1) * DV_D] = acc_scr[mp] / l_scr[mp]


def _dattn_prompt(q, k, v, bias_tiles, *, T):
    bsz, seq, _ = q.shape
    tq = 2 * T
    kv_spec = pl.BlockSpec((1, seq, 128), lambda b, h, qi: (b, 0, h))
    return pl.pallas_call(
        functools.partial(_dattn_body, T=T),
        grid=(bsz, H_D, seq // tq),
        in_specs=[pl.BlockSpec((1, tq, 128), lambda b, h, qi: (b, qi, h)), kv_spec, kv_spec,
                  pl.BlockSpec((1, 2, T, T), lambda b, h, qi: (h, 0, 0, 0))],
        out_specs=pl.BlockSpec((1, tq, 2 * DV_D), lambda b, h, qi: (b, qi, h)),
        out_shape=jax.ShapeDtypeStruct((bsz, seq, H_D * 2 * DV_D), F32),
        scratch_shapes=[pltpu.VMEM((2, tq, 128), BF16), pltpu.VMEM((2, tq, LANES), F32),
                        pltpu.VMEM((2, tq, LANES), F32), pltpu.VMEM((2, tq, DV_D), F32)],
        compiler_params=_cparams(("parallel", "parallel", "arbitrary")),
        name="diff_attn_prompt",
    )(q, k, v, bias_tiles)


def _paged_phases(j, n_j, q_ref, kn_ref, vn_ref, bias_ref, k_refs, v_refs, o_ref,
                  wq_scr, m_scr, l_scr, acc_scr, *, LQ):
    P = len(k_refs)
    RH = 2 * LQ
    is_last = j == n_j - 1

    def update(s, v_heads):
        n = s.shape[1] // PAGE
        m_prev = m_scr[...]
        m_new = jnp.maximum(m_prev, jnp.max(s, axis=-1, keepdims=True))
        corr = jnp.exp2(m_prev - m_new)
        p = jnp.exp2(s - jnp.concatenate([m_new] * n, axis=1))
        l_scr[...] = l_scr[...] * corr + jnp.sum(p, axis=-1, keepdims=True)
        pb = p.astype(BF16)
        pv = [jnp.dot(pb[h * RH:(h + 1) * RH, :], v_heads[h], preferred_element_type=F32)
              for h in range(H_D)]
        acc_scr[...] = acc_scr[...] * corr + jnp.concatenate(pv, axis=0)
        m_scr[...] = m_new

    def init():
        @pl.when(j == 0)
        def _():
            q = q_ref[0]
            qt = jnp.concatenate([q] * (H_D * 2), axis=0)
            rr = lax.broadcasted_iota(jnp.int32, qt.shape, 0)
            cc = lax.broadcasted_iota(jnp.int32, qt.shape, 1)
            wq_scr[...] = jnp.where(_idiv(rr, LQ) == _idiv(cc, DK_D), qt, jnp.zeros_like(qt))
            m_scr[...] = jnp.full_like(m_scr, NEG)
            l_scr[...] = jnp.zeros_like(l_scr)
            acc_scr[...] = jnp.zeros_like(acc_scr)
            pad = jnp.zeros((PAGE - LQ, H_D * 2 * DK_D), BF16)
            kn = jnp.concatenate([kn_ref[0], pad], axis=0)
            vn = jnp.concatenate([vn_ref[0], pad], axis=0)
            update(_nt_dot(wq_scr[...], kn) + bias_ref[1],
                   [vn[:, h * DV_D:(h + 1) * DV_D] for h in range(H_D)])

    def main():
        kt = jnp.concatenate([k_refs[i][...].astype(BF16) for i in range(P)], axis=1)
        s = jnp.dot(wq_scr[...], kt, preferred_element_type=F32)
        last_bias = bias_ref[0] * jnp.where(is_last, 1.0, 0.0)
        s = jnp.concatenate([s[:, :(P - 1) * PAGE], s[:, (P - 1) * PAGE:] + last_bias], axis=1)
        v_heads = [jnp.concatenate([v_refs[i][pl.ds(h, PAGE, stride=H_D), :].astype(BF16)
                                    for i in range(P)], axis=0) for h in range(H_D)]
        update(s, v_heads)

    def final():
        @pl.when(is_last)
        def _():
            o_ref[0] = acc_scr[...] / l_scr[...]

    return init, main, final


def _paged_gla_body(pt_ref, q_ref, kn_ref, vn_ref, bias_ref, kpool_ref, vpool_ref, *refs,
                    P, LQ, n_j, n_tb, C, n_chunks):
    gla_in = refs[:7]
    o_ref, go_ref, gs_ref = refs[7:10]
    paged_scr = refs[10:14]
    gla_scr = refs[14:17]
    kbuf, vbuf, sem = refs[17:20]
    s = pl.program_id(0)
    steps = pl.num_programs(0)
    slot = lax.rem(s, 2)

    def page_copies(step, slot_, known_pages):
        copies = []
        for i in range(P):
            page = pt_ref[step * P + i] if known_pages else 0
            copies.append(pltpu.make_async_copy(kpool_ref.at[page], kbuf.at[slot_, i], sem.at[slot_, 0]))
            copies.append(pltpu.make_async_copy(vpool_ref.at[page], vbuf.at[slot_, i], sem.at[slot_, 1]))
        return copies

    @pl.when(s == 0)
    def _():
        for c in page_copies(0, 0, True):
            c.start()

    gla = _gla_phases(lax.rem(s, n_tb), n_tb, *gla_in, go_ref, gs_ref, *gla_scr, C=C, n_chunks=n_chunks,
                      unroll=n_chunks)
    paged = _paged_phases(lax.rem(s, n_j), n_j, q_ref, kn_ref, vn_ref, bias_ref,
                          [kbuf.at[slot, i] for i in range(P)], [vbuf.at[slot, i] for i in range(P)],
                          o_ref, *paged_scr, LQ=LQ)
    gla[0]()
    paged[0]()
    for c in page_copies(lax.rem(s + 1, steps), 1 - slot, True):
        c.start()
    gla[1]()
    for c in page_copies(s, slot, False):
        c.wait()
    paged[1]()
    gla[2]()
    paged[2]()

    @pl.when(s == steps - 1)
    def _():
        for c in page_copies(0, 1 - slot, False):
            c.wait()


def _paged_attn_with_gla(q, k_new, v_new, kt_pool, v_pool, page_table, bias_tiles,
                         gq, gk, gv, glf, gs0, *, P):
    bd, lq, _ = q.shape
    n_pages = page_table.shape[1]
    n_j = n_pages // P
    steps = bd * n_j
    bsz, seq, _ = gq.shape
    C = math.gcd(seq, GLA_CHUNK)
    tb = bsz * seq // steps
    assert tb * steps == bsz * seq and tb % C == 0 and seq % tb == 0
    n_tb = seq // tb
    tri, g2 = _gla_consts(tb, C)
    R = H_D * 2 * lq
    assert n_pages == n_j * P
    tok = pl.BlockSpec((1, lq, 512), lambda s, pt: (s // n_j, 0, 0))
    const = lambda shape: pl.BlockSpec(shape, lambda s, pt: (0,) * len(shape))
    pool = pl.BlockSpec(memory_space=pl.ANY)
    gtok = lambda w: pl.BlockSpec((1, tb, w), lambda s, pt: (s // n_tb, s % n_tb, 0))
    gst = pl.BlockSpec((1, H_G, DK_G, DV_G), lambda s, pt: (s // n_tb, 0, 0, 0))
    page_buf = pltpu.VMEM((2, P) + kt_pool.shape[1:], F32)
    grid_spec = pltpu.PrefetchScalarGridSpec(
        num_scalar_prefetch=1,
        grid=(steps,),
        in_specs=[tok, tok, tok, const((2, R, PAGE)), pool, pool,
                  gtok(256), gtok(256), gtok(512), gtok(256), gst, const((tb, tb)), const((256, 512))],
        out_specs=[pl.BlockSpec((1, R, DV_D), lambda s, pt: (s // n_j, 0, 0)), gtok(512), gst],
        scratch_shapes=[pltpu.VMEM((R, 512), BF16), pltpu.VMEM((R, LANES), F32),
                        pltpu.VMEM((R, LANES), F32), pltpu.VMEM((R, DV_D), F32)] + _gla_scratch(tb)
        + [page_buf, page_buf, pltpu.SemaphoreType.DMA((2, 2))],
    )
    return pl.pallas_call(
        functools.partial(_paged_gla_body, P=P, LQ=lq, n_j=n_j, n_tb=n_tb, C=C, n_chunks=tb // C),
        grid_spec=grid_spec,
        out_shape=[jax.ShapeDtypeStruct((bd, R, DV_D), F32),
                   jax.ShapeDtypeStruct((bsz, seq, 512), F32),
                   jax.ShapeDtypeStruct((bsz, H_G, DK_G, DV_G), F32)],
        compiler_params=_cparams(("arbitrary",)),
        name="paged_attn_gla",
    )(page_table.reshape(-1), q, k_new, v_new, bias_tiles, kt_pool, v_pool,
      gq, gk, gv, glf, gs0, tri, g2)


def _mixout_math(x, og, rg, od, lam4, gn, sn, wo, lam_init):
    lam = (jnp.exp(jnp.sum(lam4[0:1, :] * lam4[1:2, :], axis=-1, keepdims=True))
           - jnp.exp(jnp.sum(lam4[2:3, :] * lam4[3:4, :], axis=-1, keepdims=True)) + lam_init)
    parts = []
    for h in range(H_G):
        t = og[:, h * DV_G:(h + 1) * DV_G]
        r = rg[:, h * DV_G:(h + 1) * DV_G]
        parts.append(_rms_full(t, gn) * (r * jax.nn.sigmoid(r)))
    for h in range(H_D):
        d = od[:, h * 2 * DV_D:h * 2 * DV_D + DV_D] - lam * od[:, h * 2 * DV_D + DV_D:(h + 1) * 2 * DV_D]
        parts.append(_rms_full(d, sn) * (1.0 - lam_init))
    cat = jnp.concatenate(parts, axis=-1).astype(BF16)
    return x + jnp.dot(cat, wo, preferred_element_type=F32)


def _mixout_body(x_ref, og_ref, rg_ref, od_ref, lam4_ref, gn_ref, sn_ref, wo_ref, o_ref, *, lam_init):
    o_ref[...] = _mixout_math(x_ref[...], og_ref[...], rg_ref[...], od_ref[...], lam4_ref[...],
                              gn_ref[...], sn_ref[...], wo_ref[...], lam_init)


def _mix_xattn_body(x_ref, og_ref, rg_ref, od_ref, lam4_ref, gn_ref, sn_ref, wo_ref,
                    xg_ref, wq_ref, qn_ref, mk_ref, mv_ref, wox_ref, o_ref, *, lam_init, dh):
    x = _mixout_math(x_ref[...], og_ref[...], rg_ref[...], od_ref[...], lam4_ref[...],
                     gn_ref[...], sn_ref[...], wo_ref[...], lam_init)
    h = _rms_full(x, xg_ref[...]).astype(BF16)
    y = jnp.dot(h, wq_ref[...], preferred_element_type=F32)
    qn = qn_ref[...]
    k = mk_ref[0]
    v = mv_ref[0]
    outs = []
    for i in range(H_X):
        sl = slice(i * dh, (i + 1) * dh)
        q = (_rms_full(y[:, sl], qn) * (dh ** -0.5)).astype(BF16)
        s = _nt_dot(q, k[:, sl])
        p = jnp.exp(s - jnp.max(s, axis=-1, keepdims=True))
        p = p / jnp.sum(p, axis=-1, keepdims=True)
        outs.append(jnp.dot(p.astype(BF16), v[:, sl], preferred_element_type=F32))
    att = jnp.concatenate(outs, axis=-1).astype(BF16)
    o_ref[...] = x + jnp.dot(att, wox_ref[...], preferred_element_type=F32)


def _mix_xattn(x, og, rg, od, lam4, gn, sn, wo, xg, wq, qn, mk, mv, wox, *, tm, seq, lam_init):
    n, d = x.shape
    nt = seq // tm
    resident = lambda shape: pl.BlockSpec(shape, lambda i: (0,) * len(shape), pipeline_mode=pl.Buffered(1))
    row = lambda w: pl.BlockSpec((tm, w), lambda i: (i, 0))
    mem = pl.BlockSpec((1,) + mk.shape[1:], lambda i: (i // nt, 0, 0))
    return pl.pallas_call(
        functools.partial(_mix_xattn_body, lam_init=lam_init, dh=d // H_X),
        grid=(n // tm,),
        in_specs=[row(d), row(512), row(512), row(1024), resident((4, DK_D)), resident((1, DV_G)),
                  resident((1, DV_D)), resident(wo.shape), resident((1, d)), resident(wq.shape),
                  resident(qn.shape), mem, mem, resident(wox.shape)],
        out_specs=row(d),
        out_shape=jax.ShapeDtypeStruct((n, d), F32),
        compiler_params=_cparams(("parallel",)),
        name="mixer_out_xattn",
    )(x, og, rg, od, lam4, gn, sn, wo, xg, wq, qn, mk, mv, wox)


def _mixout(x, og, rg, od, lam4, gn, sn, wo, *, tm, lam_init):
    n, d = x.shape
    full = lambda shape: pl.BlockSpec(shape, lambda i: (0, 0))
    row = lambda w: pl.BlockSpec((tm, w), lambda i: (i, 0))
    return pl.pallas_call(
        functools.partial(_mixout_body, lam_init=lam_init),
        grid=(n // tm,),
        in_specs=[row(d), row(512), row(512), row(1024), full((4, DK_D)), full((1, DV_G)),
                  full((1, DV_D)), full(wo.shape)],
        out_specs=row(d),
        out_shape=jax.ShapeDtypeStruct((n, d), F32),
        compiler_params=_cparams(("parallel",)),
        name="mixer_out",
    )(x, og, rg, od, lam4, gn, sn, wo)


def _proj_body(x_ref, g_ref, w_ref, hn_ref, *o_refs, head_norm, scale, dh):
    h = _rms_full(x_ref[...], g_ref[...]).astype(BF16)
    y = jnp.dot(h, w_ref[...], preferred_element_type=F32)
    if head_norm:
        hn = hn_ref[...]
        y = jnp.concatenate(
            [_rms_full(y[:, i * dh:(i + 1) * dh], hn) for i in range(y.shape[1] // dh)], axis=-1)
    y = y * scale
    for o_ref in o_refs:
        o_ref[...] = y.astype(o_ref.dtype)


def _proj(x, g, w, hn, *, tm, head_norm, scale, out_dtypes):
    n, d = x.shape
    dout = w.shape[1]
    dh = hn.shape[1]
    full = lambda shape: pl.BlockSpec(shape, lambda i: (0, 0))
    row = lambda wd: pl.BlockSpec((tm, wd), lambda i: (i, 0))
    return pl.pallas_call(
        functools.partial(_proj_body, head_norm=head_norm, scale=scale, dh=dh),
        grid=(n // tm,),
        in_specs=[row(d), full((1, d)), full(w.shape), full((1, dh))],
        out_specs=[row(dout) for _ in out_dtypes],
        out_shape=[jax.ShapeDtypeStruct((n, dout), dt) for dt in out_dtypes],
        compiler_params=_cparams(("parallel",)),
        name="proj",
    )(x, g, w, hn)


def _xattn_cached_body(q_ref, k_ref, v_ref, o_ref, *, dh, M, G):
    nlb = dh // LANES
    stride = nlb * H_X
    lq = q_ref.shape[1]
    s_rows = []
    for g in range(G):
        q = q_ref[g]
        for h in range(H_X):
            s = None
            for lb in range(nlb):
                kh = k_ref[g, pl.ds(lb * H_X + h, M, stride=stride), :].astype(BF16)
                t = _nt_dot(q[:, h * dh + lb * LANES:h * dh + (lb + 1) * LANES], kh)
                s = t if s is None else s + t
            s_rows.append(s)
    s = jnp.concatenate(s_rows, axis=0)
    m = jnp.max(s, axis=-1, keepdims=True)
    p = jnp.exp(s - m)
    p = p / jnp.sum(p, axis=-1, keepdims=True)
    for g in range(G):
        outs = []
        for h in range(H_X):
            r0 = (g * H_X + h) * lq
            ph = p[r0:r0 + lq, :].astype(BF16)
            for lb in range(nlb):
                vh = v_ref[g, pl.ds(lb * H_X + h, M, stride=stride), :].astype(BF16)
                outs.append(jnp.dot(ph, vh, preferred_element_type=F32))
        o_ref[g] = jnp.concatenate(outs, axis=-1).astype(o_ref.dtype)


def _xattn_cached(q, k_rows, v_rows, *, M, G):
    bsz, lq, d = q.shape
    kv = pl.BlockSpec((G, k_rows.shape[1], LANES), lambda b: (b, 0, 0))
    return pl.pallas_call(
        functools.partial(_xattn_cached_body, dh=d // H_X, M=M, G=G),
        grid=(bsz // G,),
        in_specs=[pl.BlockSpec((G, lq, d), lambda b: (b, 0, 0)), kv, kv],
        out_specs=pl.BlockSpec((G, lq, d), lambda b: (b, 0, 0)),
        out_shape=jax.ShapeDtypeStruct((bsz, lq, d), BF16),
        compiler_params=_cparams(("parallel",)),
        name="xattn_cached",
    )(q, k_rows, v_rows)


def _resproj_body(x_ref, a_ref, w_ref, o_ref):
    o_ref[...] = x_ref[...] + jnp.dot(a_ref[...], w_ref[...], preferred_element_type=F32)


def _resproj(x, a, w, *, tm):
    n, d = x.shape
    row = lambda wd: pl.BlockSpec((tm, wd), lambda i: (i, 0))
    return pl.pallas_call(
        _resproj_body,
        grid=(n // tm,),
        in_specs=[row(d), row(a.shape[1]), pl.BlockSpec(w.shape, lambda i: (0, 0))],
        out_specs=row(d),
        out_shape=jax.ShapeDtypeStruct((n, d), F32),
        compiler_params=_cparams(("parallel",)),
        name="resproj",
    )(x, a, w)


def kernel(x_prompt, x_sample, mem_prompt, cache_diff_k, cache_diff_v, state_gla, cache_mem_k, cache_mem_v, page_table, rel_bias, ffn1_norm, ffn1_w_gate, ffn1_w_up, ffn1_w_down, mix_norm, w_in, gla_w_gate_up, gla_b_gate, gla_out_norm, diff_q_norm, diff_k_norm, diff_lam_q1, diff_lam_k1, diff_lam_q2, diff_lam_k2, diff_subln, w_out, xattn_norm, mem_norm, xattn_wq, xattn_wk, xattn_wv, xattn_q_norm, xattn_k_norm, xattn_wo, ffn2_norm, ffn2_w_gate, ffn2_w_up, ffn2_w_down):
    depth = ffn1_norm.shape[0]
    B, L, D = x_prompt.shape
    Bd, Ld, _ = x_sample.shape
    M = mem_prompt.shape[1]
    n_pool = cache_diff_k.shape[1]
    T = 512
    TM = 512
    P_PAGES = 32

    xp = x_prompt.reshape(B * L, D)
    xs = x_sample.reshape(Bd * Ld, D)
    outs = {k: [] for k in ("kp", "vp", "ks", "vs", "sp", "ss", "mk", "mv")}

    g64 = jnp.asarray(np.kron(np.eye(512 // DK_D, dtype=np.float32), np.ones((DK_D, DK_D), np.float32)), BF16)

    for l in range(depth):
        lam_init = 0.8 - 0.6 * math.exp(-0.3 * l)
        row = lambda a: a[l].reshape(1, -1)

        w1g, w1u, w1d = (w[l].astype(BF16) for w in (ffn1_w_gate, ffn1_w_up, ffn1_w_down))
        w2g, w2u, w2d = (w[l].astype(BF16) for w in (ffn2_w_gate, ffn2_w_up, ffn2_w_down))
        wi = w_in[l]
        lr0 = 2 * H_G * DK_G + 2 * H_G * DV_G
        w_main = jnp.concatenate([wi[:, :lr0], wi[:, lr0 + GLA_RANK:]], axis=1).astype(BF16)
        w_lr = jnp.pad(wi[:, lr0:lr0 + GLA_RANK], ((0, 0), (0, 128 - GLA_RANK))).astype(BF16)
        w_up = jnp.pad(gla_w_gate_up[l], ((0, 128 - GLA_RANK), (0, 0))).astype(BF16)
        qn = jnp.tile(diff_q_norm[l], 2 * H_D).reshape(1, -1)
        kn = jnp.tile(diff_k_norm[l], 2 * H_D).reshape(1, -1)
        lam4 = jnp.stack([diff_lam_q1[l], diff_lam_k1[l], diff_lam_q2[l], diff_lam_k2[l]]).astype(F32)
        wo_mix = w_out[l].astype(BF16)
        wq_x, wk_x, wv_x, wo_x = (w[l].astype(BF16) for w in (xattn_wq, xattn_wk, xattn_wv, xattn_wo))

        tab = rel_bias.astype(F32).T.reshape(-1)
        bias_tiles = _bias_prompt_tiles(tab, T=T)
        bias_paged = _bias_paged_tiles(tab, LQ=Ld)

        xp = _ffn(xp, row(ffn1_norm), w1g, w1u, w1d, tm=1024, tf=256)
        xs = _ffn(xs, row(ffn1_norm), w1g, w1u, w1d, tm=1024, tf=256)

        mix_args = (row(mix_norm), w_main, w_lr, w_up, row(gla_b_gate), qn, kn, g64)
        qg, kg, vg, rg, lf, qd, kdt, kdb, vd, vdb = _mixin(xp, *mix_args, tm=TM, seq=L)
        qg_s, kg_s, vg_s, rg_s, lf_s, qd_s, kd_s, kdb_s, vd_s, vdb_s = _mixin(xs, *mix_args, tm=TM)
        b3 = lambda a: a.reshape(B, L, -1)
        s3 = lambda a: a.reshape(Bd, Ld, -1)
        og_s, ss = _gla(s3(qg_s), s3(kg_s), s3(vg_s), s3(lf_s), state_gla[l], tb=Ld, G=4)

        kt_pool = jnp.transpose(cache_diff_k[l], (0, 2, 3, 4, 1)).reshape(n_pool, H_D * 2 * DK_D, PAGE)
        v_pool = cache_diff_v[l].reshape(n_pool, PAGE * H_D, DV_D)
        s0 = jnp.zeros((B, H_G, DK_G, DV_G), F32)
        od_s, og, sp = _paged_attn_with_gla(s3(qd_s), s3(kdb_s), s3(vdb_s), kt_pool, v_pool, page_table,
                                            bias_paged, b3(qg), b3(kg), b3(vg), b3(lf), s0, P=P_PAGES)

        od = _dattn_prompt(b3(qd), b3(kdb), b3(vdb), bias_tiles, T=T)
        dh = D // H_X
        mem2 = mem_prompt.reshape(B * M, D)
        mk, mkb = _proj(mem2, row(mem_norm), wk_x, row(xattn_k_norm), tm=TM, head_norm=True, scale=1.0,
                        out_dtypes=(F32, BF16))
        mv, mvb = _proj(mem2, row(mem_norm), wv_x, jnp.ones((1, dh), F32), tm=TM, head_norm=False,
                        scale=1.0, out_dtypes=(F32, BF16))
        xp = _mix_xattn(xp, og.reshape(B * L, -1), rg, od.reshape(B * L, -1), lam4, row(gla_out_norm),
                        row(diff_subln), wo_mix, row(xattn_norm), wq_x, row(xattn_q_norm),
                        mkb.reshape(B, M, D), mvb.reshape(B, M, D), wo_x, tm=TM, seq=L, lam_init=lam_init)
        outs["mk"].append(mk.reshape(B, M, H_X, dh))
        outs["mv"].append(mv.reshape(B, M, H_X, dh))
        outs["kp"].append(kdt.reshape(B, H_D, 2, DK_D, L).transpose(0, 4, 1, 2, 3))
        outs["vp"].append(vd.reshape(B, L, H_D, DV_D))
        outs["sp"].append(sp)

        od_s = od_s.reshape(Bd, H_D, 2, Ld, DV_D).transpose(0, 3, 1, 2, 4).reshape(Bd * Ld, -1)
        xs = _mixout(xs, og_s.reshape(Bd * Ld, -1), rg_s, od_s, lam4, row(gla_out_norm), row(diff_subln),
                     wo_mix, tm=TM, lam_init=lam_init)
        outs["ks"].append(kd_s.reshape(Bd, Ld, H_D, 2, DK_D))
        outs["vs"].append(vd_s.reshape(Bd, Ld, H_D, DV_D))
        outs["ss"].append(ss)

        nlb = dh // LANES
        (qs,) = _proj(xs, row(xattn_norm), wq_x, row(xattn_q_norm), tm=TM, head_norm=True,
                      scale=dh ** -0.5, out_dtypes=(BF16,))
        cache_rows = lambda c: (c[l].reshape(Bd, M, H_X, nlb, LANES).transpose(0, 1, 3, 2, 4)
                                .reshape(Bd, M * nlb * H_X, LANES))
        osm = _xattn_cached(qs.reshape(Bd, Ld, D), cache_rows(cache_mem_k), cache_rows(cache_mem_v), M=M,
                            G=4)
        xs = _resproj(xs, osm.reshape(Bd * Ld, D), wo_x, tm=TM)

        xp = _ffn(xp, row(ffn2_norm), w2g, w2u, w2d, tm=1024, tf=256)
        xs = _ffn(xs, row(ffn2_norm), w2g, w2u, w2d, tm=1024, tf=256)

    st = lambda key: jnp.stack(outs[key])
    return (xp.reshape(B, L, D), xs.reshape(Bd, Ld, D), st("kp"), st("vp"), st("ks"), st("vs"),
            st("sp"), st("ss"), st("mk"), st("mv"))
```

```python
import functools
import math

import numpy as np
import jax
import jax.numpy as jnp
from jax import lax
from jax.experimental import pallas as pl
from jax.experimental.pallas import tpu as pltpu

F32 = jnp.float32
BF16 = jnp.bfloat16

EPS = 1e-6
NEG = -1e30
LOG2E = 1.0 / math.log(2.0)

H_G, DK_G, DV_G = 4, 64, 128
GLA_RANK = 16
GLA_GATE_NORM = 16.0
GLA_CHUNK = 16
H_D, DK_D, DV_D = 4, 64, 128
N_BUCKETS = 32
T5_MAX_EXACT = N_BUCKETS // 2
T5_MAX_DIST = 128
H_X = 4
PAGE = 128
LANES = 128

VMEM_LIMIT = 56 * 1024 * 1024


def _cparams(sem):
    return pltpu.CompilerParams(dimension_semantics=sem, vmem_limit_bytes=VMEM_LIMIT)


def _rms_full(x, g):
    ms = jnp.mean(x * x, axis=-1, keepdims=True)
    return x * lax.rsqrt(ms + EPS) * g


def _idiv(x, d):
    if d & (d - 1) == 0:
        return lax.shift_right_logical(x, int(math.log2(d)))
    return x // d


def _nt_dot(a, b):
    return lax.dot_general(a, b, (((1,), (1,)), ((), ())), preferred_element_type=F32)


def _ffn_body(x_ref, g_ref, wg_ref, wu_ref, wd_ref, o_ref, h_scr, acc_scr, *, tf):
    h_scr[...] = _rms_full(x_ref[...], g_ref[...]).astype(BF16)
    acc_scr[...] = jnp.zeros_like(acc_scr)

    def chunk(j, carry):
        c0 = pl.multiple_of(j * tf, tf)
        h = h_scr[...]
        gate = jnp.dot(h, wg_ref[:, pl.ds(c0, tf)], preferred_element_type=F32)
        up = jnp.dot(h, wu_ref[:, pl.ds(c0, tf)], preferred_element_type=F32)
        a = (gate * jax.nn.sigmoid(gate) * up).astype(BF16)
        acc_scr[...] += jnp.dot(a, wd_ref[pl.ds(c0, tf), :], preferred_element_type=F32)
        return carry

    lax.fori_loop(0, wg_ref.shape[1] // tf, chunk, 0)
    o_ref[...] = x_ref[...] + 0.5 * acc_scr[...]


def _ffn(x, g, wg, wu, wd, *, tm, tf):
    n, d = x.shape
    resident = lambda shape: pl.BlockSpec(shape, lambda i: (0, 0), pipeline_mode=pl.Buffered(1))
    return pl.pallas_call(
        functools.partial(_ffn_body, tf=tf),
        grid=(n // tm,),
        in_specs=[pl.BlockSpec((tm, d), lambda i: (i, 0)), resident((1, d)),
                  resident(wg.shape), resident(wu.shape), resident(wd.shape)],
        out_specs=pl.BlockSpec((tm, d), lambda i: (i, 0)),
        out_shape=jax.ShapeDtypeStruct((n, d), F32),
        scratch_shapes=[pltpu.VMEM((tm, d), BF16), pltpu.VMEM((tm, d), F32)],
        compiler_params=_cparams(("parallel",)),
        name="ffn",
    )(x, g, wg, wu, wd)


def _mixin_body(x_ref, g_ref, w_ref, wlr_ref, wup_ref, bg_ref, qn_ref, kn_ref, g64_ref,
                qg_ref, kg_ref, vg_ref, rg_ref, lf_ref, qd_ref, kd_ref, kdb_ref, vd_ref, vdb_ref,
                *, k_transposed):
    h = _rms_full(x_ref[...], g_ref[...]).astype(BF16)
    y = jnp.dot(h, w_ref[...], preferred_element_type=F32)
    qg_ref[...] = y[:, 0:256] * (DK_G ** -0.5)
    kg_ref[...] = y[:, 256:512]
    vg_ref[...] = y[:, 512:1024]
    rg_ref[...] = y[:, 1024:1536]
    qd = y[:, 1536:2048]
    kd = y[:, 2048:2560]
    vd = y[:, 2560:3072]

    glr = jnp.dot(h, wlr_ref[...], preferred_element_type=F32)
    z = jnp.dot(glr.astype(BF16), wup_ref[...], preferred_element_type=F32) + bg_ref[...]
    lf_ref[...] = (jnp.minimum(z, 0.0) - jnp.log(1.0 + jnp.exp(-jnp.abs(z)))) * (1.0 / GLA_GATE_NORM)

    def group_rms(t, gain):
        msq = jnp.dot((t * t).astype(BF16), g64_ref[...], preferred_element_type=F32) * (1.0 / DK_D)
        return t * lax.rsqrt(msq + EPS) * gain

    qdn = group_rms(qd, qn_ref[...]) * (DK_D ** -0.5 * LOG2E)
    kdn = group_rms(kd, kn_ref[...])
    qd_ref[...] = qdn.astype(BF16)
    if k_transposed:
        kd_ref[0] = kdn.T
    else:
        kd_ref[...] = kdn
    kdb_ref[...] = kdn.astype(BF16)
    for h in range(H_D):
        vd_ref[pl.ds(h, vd.shape[0], stride=H_D), :] = vd[:, h * DV_D:(h + 1) * DV_D]
    vdb_ref[...] = vd.astype(BF16)


def _mixin(x, g, w_main, w_lr, w_up, b_gate, qn, kn, g64, *, tm, seq=None):
    n, d = x.shape
    full = lambda shape: pl.BlockSpec(shape, lambda i: (0, 0))
    row = lambda w: pl.BlockSpec((tm, w), lambda i: (i, 0))
    outs = [(256, F32), (256, F32), (512, F32), (512, F32), (256, F32),
            (512, BF16), (512, F32), (512, BF16), (512, F32), (512, BF16)]
    out_specs = [row(w) for w, _ in outs]
    out_shape = [jax.ShapeDtypeStruct((n, w), dt) for w, dt in outs]
    if seq is not None:
        nt = seq // tm
        out_specs[6] = pl.BlockSpec((1, 512, tm), lambda i: (i // nt, 0, i % nt))
        out_shape[6] = jax.ShapeDtypeStruct((n // seq, 512, seq), F32)
    out_specs[8] = pl.BlockSpec((tm * H_D, DV_D), lambda i: (i, 0))
    out_shape[8] = jax.ShapeDtypeStruct((n * H_D, DV_D), F32)
    return pl.pallas_call(
        functools.partial(_mixin_body, k_transposed=seq is not None),
        grid=(n // tm,),
        in_specs=[row(d), full((1, d)), full(w_main.shape), full(w_lr.shape), full(w_up.shape),
                  full((1, 256)), full((1, 512)), full((1, 512)), full((512, 512))],
        out_specs=out_specs,
        out_shape=out_shape,
        compiler_params=_cparams(("parallel",)),
        name="mixer_in",
    )(x, g, w_main, w_lr, w_up, b_gate, qn, kn, g64)


def _gla_phases(tb, n_tb, q_ref, k_ref, v_ref, lf_ref, s0_ref, tri_ref, g2_ref, o_ref, so_ref,
                st_scr, nat_scr, b_scr, *, C, n_chunks, unroll):
    def init():
        @pl.when(tb == 0)
        def _():
            nat_scr[...] = jnp.zeros_like(nat_scr)
            for h in range(H_G):
                nat_scr[h * DK_G:(h + 1) * DK_G, h * DV_G:(h + 1) * DV_G] = s0_ref[0, h]
            st_scr[...] = nat_scr[...].T

    def final():
        @pl.when(tb == n_tb - 1)
        def _():
            nat = st_scr[...].T
            for h in range(H_G):
                so_ref[0, h] = nat[h * DK_G:(h + 1) * DK_G, h * DV_G:(h + 1) * DV_G]

    masks = {}

    def chunk(c, carry):
        head_mask = masks["head"]
        t_idx = masks["t"]
        r0 = pl.multiple_of(c * C, C)
        q = q_ref[0, pl.ds(r0, C), :]
        k = k_ref[0, pl.ds(r0, C), :]
        v = v_ref[0, pl.ds(r0, C), :]
        b = b_scr[pl.ds(r0, C), :]
        st = st_scr[...]

        o = _nt_dot((q * jnp.exp(b)).astype(BF16), st.astype(BF16))

        pieces = []
        for s in range(C):
            rel = jnp.where(t_idx >= s, b - b[s:s + 1, :], NEG)
            pieces.append(q * k[s:s + 1, :] * jnp.exp(rel))
        w = jnp.concatenate(pieces, axis=0).astype(BF16)
        a = jnp.dot(w, g2_ref[...], preferred_element_type=F32)
        for s in range(C):
            o = o + a[s * C:(s + 1) * C, :] * v[s:s + 1, :]
        o_ref[0, pl.ds(r0, C), :] = o

        b_last = b[C - 1:C, :]
        kt = (k * jnp.exp(b_last - b)).astype(BF16)
        upd = lax.dot_general(v.astype(BF16), kt, (((0,), (0,)), ((), ())),
                              preferred_element_type=F32)
        st_scr[...] = st * jnp.exp(b_last) + jnp.where(head_mask, upd, 0.0)
        return carry

    def main():
        rows = lax.broadcasted_iota(jnp.int32, (H_G * DV_G, H_G * DK_G), 0)
        cols = lax.broadcasted_iota(jnp.int32, (H_G * DV_G, H_G * DK_G), 1)
        masks["head"] = _idiv(rows, DV_G) == _idiv(cols, DK_G)
        masks["t"] = lax.broadcasted_iota(jnp.int32, (C, H_G * DK_G), 0)
        b_scr[...] = jnp.dot(tri_ref[...], lf_ref[0], preferred_element_type=F32,
                             precision=lax.Precision.HIGHEST)
        lax.fori_loop(0, n_chunks, chunk, 0, unroll=unroll)

    return init, main, final


def _gla_body(*refs, C, n_chunks, G):
    q_ref, k_ref, v_ref, lf_ref, s0_ref, tri_ref, g2_ref, o_ref, so_ref = refs[:9]
    scratch = refs[9:]
    per_seq = []
    for g in range(G):
        one = lambda r, g=g: r.at[pl.ds(g, 1)]
        per_seq.append(_gla_phases(
            pl.program_id(1), pl.num_programs(1), one(q_ref), one(k_ref), one(v_ref), one(lf_ref),
            one(s0_ref), tri_ref, g2_ref, one(o_ref), one(so_ref), *[s.at[g] for s in scratch],
            C=C, n_chunks=n_chunks, unroll=min(2, n_chunks)))
    for phases in zip(*per_seq):
        for phase in phases:
            phase()


def _gla_consts(tb, C):
    tri = np.kron(np.eye(tb // C, dtype=np.float32), np.tril(np.ones((C, C), np.float32)))
    g2 = np.kron(np.eye(H_G, dtype=np.float32), np.ones((DK_G, DV_G), np.float32))
    return jnp.asarray(tri), jnp.asarray(g2, BF16)


def _gla_scratch(tb, lead=()):
    return [pltpu.VMEM(lead + (H_G * DV_G, H_G * DK_G), F32), pltpu.VMEM(lead + (H_G * DK_G, H_G * DV_G), F32),
            pltpu.VMEM(lead + (tb, H_G * DK_G), F32)]


def _gla(q, k, v, lf, s0, *, tb, G):
    bsz, seq, _ = q.shape
    C = math.gcd(seq, GLA_CHUNK)
    tb = min(tb, seq)
    tri, g2 = _gla_consts(tb, C)
    tok = lambda w: pl.BlockSpec((G, tb, w), lambda b, t: (b, t, 0))
    st_spec = pl.BlockSpec((G, H_G, DK_G, DV_G), lambda b, t: (b, 0, 0, 0))
    return pl.pallas_call(
        functools.partial(_gla_body, C=C, n_chunks=tb // C, G=G),
        grid=(bsz // G, seq // tb),
        in_specs=[tok(256), tok(256), tok(512), tok(256), st_spec,
                  pl.BlockSpec((tb, tb), lambda b, t: (0, 0)),
                  pl.BlockSpec((256, 512), lambda b, t: (0, 0))],
        out_specs=[tok(512), st_spec],
        out_shape=[jax.ShapeDtypeStruct((bsz, seq, 512), F32),
                   jax.ShapeDtypeStruct((bsz, H_G, DK_G, DV_G), F32)],
        scratch_shapes=_gla_scratch(tb, (G,)),
        compiler_params=_cparams(("parallel", "arbitrary")),
        name="gla",
    )(q, k, v, lf, s0, tri, g2)


def _t5_shifted_bias(n, tab_ref, h):
    nf = jnp.maximum(n, 1).astype(F32)
    large = T5_MAX_EXACT + (jnp.log(nf / T5_MAX_EXACT) / math.log(T5_MAX_DIST / T5_MAX_EXACT)
                            * (N_BUCKETS - T5_MAX_EXACT)).astype(jnp.int32)
    large = jnp.minimum(large, N_BUCKETS - 1)
    bucket = jnp.where(n < T5_MAX_EXACT, n, large)
    far = tab_ref[h * N_BUCKETS + N_BUCKETS - 1]
    val = jnp.zeros(n.shape, F32)
    for b in range(N_BUCKETS - 1):
        val = jnp.where(bucket == b, (tab_ref[h * N_BUCKETS + b] - far) * LOG2E, val)
    return val


def _bias_prompt_body(tab_ref, o_ref, *, T, RB):
    h = pl.program_id(0)
    t = pl.program_id(1)
    for r0 in range(0, T, RB):
        blocks = [c0 for c0 in range(0, T, LANES)
                  if (r0 - (c0 + LANES - 1) < T5_MAX_DIST and r0 + RB - 1 - c0 >= 0)
                  or T + r0 - (c0 + LANES - 1) < T5_MAX_DIST]
        for c0 in range(0, T, LANES):
            r = lax.broadcasted_iota(jnp.int32, (RB, LANES), 0) + r0
            c = lax.broadcasted_iota(jnp.int32, (RB, LANES), 1) + c0
            n = r - c + t * T
            if c0 in blocks:
                val = _t5_shifted_bias(jnp.maximum(n, 0), tab_ref, h)
            else:
                val = jnp.zeros((RB, LANES), F32)
            o_ref[0, 0, r0:r0 + RB, c0:c0 + LANES] = jnp.where(n >= 0, val, NEG)


def _bias_prompt_tiles(tab, *, T):
    assert T >= T5_MAX_DIST
    return pl.pallas_call(
        functools.partial(_bias_prompt_body, T=T, RB=64),
        grid=(H_D, 2),
        in_specs=[pl.BlockSpec(memory_space=pltpu.SMEM)],
        out_specs=pl.BlockSpec((1, 1, T, T), lambda h, t: (h, t, 0, 0)),
        out_shape=jax.ShapeDtypeStruct((H_D, 2, T, T), F32),
        compiler_params=_cparams(("parallel", "parallel")),
        name="bias_prompt",
    )(tab)


def _bias_paged_body(tab_ref, o_ref, *, LQ):
    t = pl.program_id(0)
    rh = 2 * LQ
    r = lax.broadcasted_iota(jnp.int32, (rh, PAGE), 0)
    c = lax.broadcasted_iota(jnp.int32, (rh, PAGE), 1)
    n = (r - _idiv(r, LQ) * LQ) - c + (1 - t) * PAGE
    for h in range(H_D):
        val = _t5_shifted_bias(jnp.maximum(n, 0), tab_ref, h)
        o_ref[0, h * rh:(h + 1) * rh, :] = jnp.where(n >= 0, val, NEG)


def _bias_paged_tiles(tab, *, LQ):
    assert PAGE >= T5_MAX_DIST
    R = H_D * 2 * LQ
    return pl.pallas_call(
        functools.partial(_bias_paged_body, LQ=LQ),
        grid=(2,),
        in_specs=[pl.BlockSpec(memory_space=pltpu.SMEM)],
        out_specs=pl.BlockSpec((1, R, PAGE), lambda t: (t, 0, 0)),
        out_shape=jax.ShapeDtypeStruct((2, R, PAGE), F32),
        compiler_params=_cparams(("parallel",)),
        name="bias_paged",
    )(tab)


def _dattn_body(q_ref, k_ref, v_ref, bias_ref, o_ref, qm_scr, m_scr, l_scr, acc_scr, *, T):
    qi = pl.program_id(2)
    m_scr[...] = jnp.full_like(m_scr, NEG)
    l_scr[...] = jnp.zeros_like(l_scr)
    acc_scr[...] = jnp.zeros_like(acc_scr)
    q = q_ref[0]
    lane = lax.broadcasted_iota(jnp.int32, q.shape, 1)
    qm_scr[0] = jnp.where(lane < DK_D, q, jnp.zeros_like(q))
    qm_scr[1] = jnp.where(lane >= DK_D, q, jnp.zeros_like(q))

    def step(kj, r0, nr, bias):
        c0 = pl.multiple_of(kj * T, T)
        k = k_ref[0, pl.ds(c0, T), :]
        v = v_ref[0, pl.ds(c0, T), :]
        for mp in range(2):
            s = _nt_dot(qm_scr[mp, r0:r0 + nr, :], k)
            if bias is not None:
                nb = bias.shape[0]
                s = s + bias if nb == nr else jnp.concatenate([s[:nb] + bias, s[nb:]], axis=0)
            m_prev = m_scr[mp, r0:r0 + nr, :]
            m_new = jnp.maximum(m_prev, jnp.max(s, axis=-1, keepdims=True))
            corr = jnp.exp2(m_prev - m_new)
            p = jnp.exp2(s - jnp.concatenate([m_new] * (T // LANES), axis=1))
            l_scr[mp, r0:r0 + nr, :] = l_scr[mp, r0:r0 + nr, :] * corr + jnp.sum(p, axis=-1, keepdims=True)
            acc_scr[mp, r0:r0 + nr, :] = (acc_scr[mp, r0:r0 + nr, :] * corr
                                          + jnp.dot(p.astype(BF16), v, preferred_element_type=F32))
            m_scr[mp, r0:r0 + nr, :] = m_new

    def far_step(kj, carry):
        step(kj, 0, 2 * T, None)
        return carry

    lax.fori_loop(0, jnp.maximum(2 * qi - 1, 0), far_step, 0)

    diag = bias_ref[0, 0]
    sub = bias_ref[0, 1]

    @pl.when(qi >= 1)
    def _():
        step(2 * qi - 1, 0, 2 * T, sub)

    step(2 * qi, 0, 2 * T, jnp.concatenate([diag, sub], axis=0))
    step(2 * qi + 1, T, T, diag)
    for mp in range(2):
        o_ref[0, :, mp * DV_D:(mp + 1) * DV_D] = acc_scr[mp] / l_scr[mp]


def _dattn_prompt(q, k, v, bias_tiles, *, T):
    bsz, seq, _ = q.shape
    tq = 2 * T
    kv_spec = pl.BlockSpec((1, seq, 128), lambda b, h, qi: (b, 0, h))
    return pl.pallas_call(
        functools.partial(_dattn_body, T=T),
        grid=(bsz, H_D, seq // tq),
        in_specs=[pl.BlockSpec((1, tq, 128), lambda b, h, qi: (b, qi, h)), kv_spec, kv_spec,
                  pl.BlockSpec((1, 2, T, T), lambda b, h, qi: (h, 0, 0, 0))],
        out_specs=pl.BlockSpec((1, tq, 2 * DV_D), lambda b, h, qi: (b, qi, h)),
        out_shape=jax.ShapeDtypeStruct((bsz, seq, H_D * 2 * DV_D), F32),
        scratch_shapes=[pltpu.VMEM((2, tq, 128), BF16), pltpu.VMEM((2, tq, LANES), F32),
                        pltpu.VMEM((2, tq, LANES), F32), pltpu.VMEM((2, tq, DV_D), F32)],
        compiler_params=_cparams(("parallel", "parallel", "arbitrary")),
        name="diff_attn_prompt",
    )(q, k, v, bias_tiles)


def _paged_phases(j, n_j, q_ref, kn_ref, vn_ref, bias_ref, k_refs, v_refs, o_ref,
                  wq_scr, m_scr, l_scr, acc_scr, *, LQ):
    P = len(k_refs)
    RH = 2 * LQ
    is_last = j == n_j - 1

    def update(s, v_heads):
        n = s.shape[1] // PAGE
        m_prev = m_scr[...]
        m_new = jnp.maximum(m_prev, jnp.max(s, axis=-1, keepdims=True))
        corr = jnp.exp2(m_prev - m_new)
        p = jnp.exp2(s - jnp.concatenate([m_new] * n, axis=1))
        l_scr[...] = l_scr[...] * corr + jnp.sum(p, axis=-1, keepdims=True)
        pb = p.astype(BF16)
        pv = [jnp.dot(pb[h * RH:(h + 1) * RH, :], v_heads[h], preferred_element_type=F32)
              for h in range(H_D)]
        acc_scr[...] = acc_scr[...] * corr + jnp.concatenate(pv, axis=0)
        m_scr[...] = m_new

    def init():
        @pl.when(j == 0)
        def _():
            q = q_ref[0]
            qt = jnp.concatenate([q] * (H_D * 2), axis=0)
            rr = lax.broadcasted_iota(jnp.int32, qt.shape, 0)
            cc = lax.broadcasted_iota(jnp.int32, qt.shape, 1)
            wq_scr[...] = jnp.where(_idiv(rr, LQ) == _idiv(cc, DK_D), qt, jnp.zeros_like(qt))
            m_scr[...] = jnp.full_like(m_scr, NEG)
            l_scr[...] = jnp.zeros_like(l_scr)
            acc_scr[...] = jnp.zeros_like(acc_scr)
            pad = jnp.zeros((PAGE - LQ, H_D * 2 * DK_D), BF16)
            kn = jnp.concatenate([kn_ref[0], pad], axis=0)
            vn = jnp.concatenate([vn_ref[0], pad], axis=0)
            update(_nt_dot(wq_scr[...], kn) + bias_ref[1],
                   [vn[:, h * DV_D:(h + 1) * DV_D] for h in range(H_D)])

    def main():
        kt = jnp.concatenate([k_refs[i][...].astype(BF16) for i in range(P)], axis=1)
        s = jnp.dot(wq_scr[...], kt, preferred_element_type=F32)
        last_bias = bias_ref[0] * jnp.where(is_last, 1.0, 0.0)
        s = jnp.concatenate([s[:, :(P - 1) * PAGE], s[:, (P - 1) * PAGE:] + last_bias], axis=1)
        v_heads = [jnp.concatenate([v_refs[i][pl.ds(h, PAGE, stride=H_D), :].astype(BF16)
                                    for i in range(P)], axis=0) for h in range(H_D)]
        update(s, v_heads)

    def final():
        @pl.when(is_last)
        def _():
            o_ref[0] = acc_scr[...] / l_scr[...]

    return init, main, final


def _paged_gla_body(pt_ref, q_ref, kn_ref, vn_ref, bias_ref, kpool_ref, vpool_ref, *refs,
                    P, LQ, n_j, n_tb, C, n_chunks):
    gla_in = refs[:7]
    o_ref, go_ref, gs_ref = refs[7:10]
    paged_scr = refs[10:14]
    gla_scr = refs[14:17]
    kbuf, vbuf, sem = refs[17:20]
    s = pl.program_id(0)
    steps = pl.num_programs(0)
    slot = lax.rem(s, 2)

    def page_copies(step, slot_, known_pages):
        copies = []
        for i in range(P):
            page = pt_ref[step * P + i] if known_pages else 0
            copies.append(pltpu.make_async_copy(kpool_ref.at[page], kbuf.at[slot_, i], sem.at[slot_, 0]))
            copies.append(pltpu.make_async_copy(vpool_ref.at[page], vbuf.at[slot_, i], sem.at[slot_, 1]))
        return copies

    @pl.when(s == 0)
    def _():
        for c in page_copies(0, 0, True):
            c.start()

    gla = _gla_phases(lax.rem(s, n_tb), n_tb, *gla_in, go_ref, gs_ref, *gla_scr, C=C, n_chunks=n_chunks,
                      unroll=n_chunks)
    paged = _paged_phases(lax.rem(s, n_j), n_j, q_ref, kn_ref, vn_ref, bias_ref,
                          [kbuf.at[slot, i] for i in range(P)], [vbuf.at[slot, i] for i in range(P)],
                          o_ref, *paged_scr, LQ=LQ)
    gla[0]()
    paged[0]()
    for c in page_copies(lax.rem(s + 1, steps), 1 - slot, True):
        c.start()
    gla[1]()
    for c in page_copies(s, slot, False):
        c.wait()
    paged[1]()
    gla[2]()
    paged[2]()

    @pl.when(s == steps - 1)
    def _():
        for c in page_copies(0, 1 - slot, False):
            c.wait()


def _paged_attn_with_gla(q, k_new, v_new, kt_pool, v_pool, page_table, bias_tiles,
                         gq, gk, gv, glf, gs0, *, P):
    bd, lq, _ = q.shape
    n_pages = page_table.shape[1]
    n_j = n_pages // P
    steps = bd * n_j
    bsz, seq, _ = gq.shape
    C = math.gcd(seq, GLA_CHUNK)
    tb = bsz * seq // steps
    assert tb * steps == bsz * seq and tb % C == 0 and seq % tb == 0
    n_tb = seq // tb
    tri, g2 = _gla_consts(tb, C)
    R = H_D * 2 * lq
    assert n_pages == n_j * P
    tok = pl.BlockSpec((1, lq, 512), lambda s, pt: (s // n_j, 0, 0))
    const = lambda shape: pl.BlockSpec(shape, lambda s, pt: (0,) * len(shape))
    pool = pl.BlockSpec(memory_space=pl.ANY)
    gtok = lambda w: pl.BlockSpec((1, tb, w), lambda s, pt: (s // n_tb, s % n_tb, 0))
    gst = pl.BlockSpec((1, H_G, DK_G, DV_G), lambda s, pt: (s // n_tb, 0, 0, 0))
    page_buf = pltpu.VMEM((2, P) + kt_pool.shape[1:], F32)
    grid_spec = pltpu.PrefetchScalarGridSpec(
        num_scalar_prefetch=1,
        grid=(steps,),
        in_specs=[tok, tok, tok, const((2, R, PAGE)), pool, pool,
                  gtok(256), gtok(256), gtok(512), gtok(256), gst, const((tb, tb)), const((256, 512))],
        out_specs=[pl.BlockSpec((1, R, DV_D), lambda s, pt: (s // n_j, 0, 0)), gtok(512), gst],
        scratch_shapes=[pltpu.VMEM((R, 512), BF16), pltpu.VMEM((R, LANES), F32),
                        pltpu.VMEM((R, LANES), F32), pltpu.VMEM((R, DV_D), F32)] + _gla_scratch(tb)
        + [page_buf, page_buf, pltpu.SemaphoreType.DMA((2, 2))],
    )
    return pl.pallas_call(
        functools.partial(_paged_gla_body, P=P, LQ=lq, n_j=n_j, n_tb=n_tb, C=C, n_chunks=tb // C),
        grid_spec=grid_spec,
        out_shape=[jax.ShapeDtypeStruct((bd, R, DV_D), F32),
                   jax.ShapeDtypeStruct((bsz, seq, 512), F32),
                   jax.ShapeDtypeStruct((bsz, H_G, DK_G, DV_G), F32)],
        compiler_params=_cparams(("arbitrary",)),
        name="paged_attn_gla",
    )(page_table.reshape(-1), q, k_new, v_new, bias_tiles, kt_pool, v_pool,
      gq, gk, gv, glf, gs0, tri, g2)


def _mixout_math(x, og, rg, od, lam4, gn, sn, wo, lam_init):
    lam = (jnp.exp(jnp.sum(lam4[0:1, :] * lam4[1:2, :], axis=-1, keepdims=True))
           - jnp.exp(jnp.sum(lam4[2:3, :] * lam4[3:4, :], axis=-1, keepdims=True)) + lam_init)
    parts = []
    for h in range(H_G):
        t = og[:, h * DV_G:(h + 1) * DV_G]
        r = rg[:, h * DV_G:(h + 1) * DV_G]
        parts.append(_rms_full(t, gn) * (r * jax.nn.sigmoid(r)))
    for h in range(H_D):
        d = od[:, h * 2 * DV_D:h * 2 * DV_D + DV_D] - lam * od[:, h * 2 * DV_D + DV_D:(h + 1) * 2 * DV_D]
        parts.append(_rms_full(d, sn) * (1.0 - lam_init))
    cat = jnp.concatenate(parts, axis=-1).astype(BF16)
    return x + jnp.dot(cat, wo, preferred_element_type=F32)


def _mixout_body(x_ref, og_ref, rg_ref, od_ref, lam4_ref, gn_ref, sn_ref, wo_ref, o_ref, *, lam_init):
    o_ref[...] = _mixout_math(x_ref[...], og_ref[...], rg_ref[...], od_ref[...], lam4_ref[...],
                              gn_ref[...], sn_ref[...], wo_ref[...], lam_init)


def _mix_xattn_body(x_ref, og_ref, rg_ref, od_ref, lam4_ref, gn_ref, sn_ref, wo_ref,
                    xg_ref, wq_ref, qn_ref, mk_ref, mv_ref, wox_ref, o_ref, *, lam_init, dh):
    x = _mixout_math(x_ref[...], og_ref[...], rg_ref[...], od_ref[...], lam4_ref[...],
                     gn_ref[...], sn_ref[...], wo_ref[...], lam_init)
    h = _rms_full(x, xg_ref[...]).astype(BF16)
    y = jnp.dot(h, wq_ref[...], preferred_element_type=F32)
    qn = qn_ref[...]
    k = mk_ref[0]
    v = mv_ref[0]
    outs = []
    for i in range(H_X):
        sl = slice(i * dh, (i + 1) * dh)
        q = (_rms_full(y[:, sl], qn) * (dh ** -0.5)).astype(BF16)
        s = _nt_dot(q, k[:, sl])
        p = jnp.exp(s - jnp.max(s, axis=-1, keepdims=True))
        p = p / jnp.sum(p, axis=-1, keepdims=True)
        outs.append(jnp.dot(p.astype(BF16), v[:, sl], preferred_element_type=F32))
    att = jnp.concatenate(outs, axis=-1).astype(BF16)
    o_ref[...] = x + jnp.dot(att, wox_ref[...], preferred_element_type=F32)


def _mix_xattn(x, og, rg, od, lam4, gn, sn, wo, xg, wq, qn, mk, mv, wox, *, tm, seq, lam_init):
    n, d = x.shape
    nt = seq // tm
    resident = lambda shape: pl.BlockSpec(shape, lambda i: (0,) * len(shape), pipeline_mode=pl.Buffered(1))
    row = lambda w: pl.BlockSpec((tm, w), lambda i: (i, 0))
    mem = pl.BlockSpec((1,) + mk.shape[1:], lambda i: (i // nt, 0, 0))
    return pl.pallas_call(
        functools.partial(_mix_xattn_body, lam_init=lam_init, dh=d // H_X),
        grid=(n // tm,),
        in_specs=[row(d), row(512), row(512), row(1024), resident((4, DK_D)), resident((1, DV_G)),
                  resident((1, DV_D)), resident(wo.shape), resident((1, d)), resident(wq.shape),
                  resident(qn.shape), mem, mem, resident(wox.shape)],
        out_specs=row(d),
        out_shape=jax.ShapeDtypeStruct((n, d), F32),
        compiler_params=_cparams(("parallel",)),
        name="mixer_out_xattn",
    )(x, og, rg, od, lam4, gn, sn, wo, xg, wq, qn, mk, mv, wox)


def _mixout(x, og, rg, od, lam4, gn, sn, wo, *, tm, lam_init):
    n, d = x.shape
    full = lambda shape: pl.BlockSpec(shape, lambda i: (0, 0))
    row = lambda w: pl.BlockSpec((tm, w), lambda i: (i, 0))
    return pl.pallas_call(
        functools.partial(_mixout_body, lam_init=lam_init),
        grid=(n // tm,),
        in_specs=[row(d), row(512), row(512), row(1024), full((4, DK_D)), full((1, DV_G)),
                  full((1, DV_D)), full(wo.shape)],
        out_specs=row(d),
        out_shape=jax.ShapeDtypeStruct((n, d), F32),
        compiler_params=_cparams(("parallel",)),
        name="mixer_out",
    )(x, og, rg, od, lam4, gn, sn, wo)


def _proj_body(x_ref, g_ref, w_ref, hn_ref, *o_refs, head_norm, scale, dh):
    h = _rms_full(x_ref[...], g_ref[...]).astype(BF16)
    y = jnp.dot(h, w_ref[...], preferred_element_type=F32)
    if head_norm:
        hn = hn_ref[...]
        y = jnp.concatenate(
            [_rms_full(y[:, i * dh:(i + 1) * dh], hn) for i in range(y.shape[1] // dh)], axis=-1)
    y = y * scale
    for o_ref in o_refs:
        o_ref[...] = y.astype(o_ref.dtype)


def _proj(x, g, w, hn, *, tm, head_norm, scale, out_dtypes):
    n, d = x.shape
    dout = w.shape[1]
    dh = hn.shape[1]
    full = lambda shape: pl.BlockSpec(shape, lambda i: (0, 0))
    row = lambda wd: pl.BlockSpec((tm, wd), lambda i: (i, 0))
    return pl.pallas_call(
        functools.partial(_proj_body, head_norm=head_norm, scale=scale, dh=dh),
        grid=(n // tm,),
        in_specs=[row(d), full((1, d)), full(w.shape), full((1, dh))],
        out_specs=[row(dout) for _ in out_dtypes],
        out_shape=[jax.ShapeDtypeStruct((n, dout), dt) for dt in out_dtypes],
        compiler_params=_cparams(("parallel",)),
        name="proj",
    )(x, g, w, hn)


def _xattn_cached_body(q_ref, k_ref, v_ref, o_ref, *, dh, M, G):
    nlb = dh // LANES
    stride = nlb * H_X
    lq = q_ref.shape[1]
    s_rows = []
    for g in range(G):
        q = q_ref[g]
        for h in range(H_X):
            s = None
            for lb in range(nlb):
                kh = k_ref[g, pl.ds(lb * H_X + h, M, stride=stride), :].astype(BF16)
                t = _nt_dot(q[:, h * dh + lb * LANES:h * dh + (lb + 1) * LANES], kh)
                s = t if s is None else s + t
            s_rows.append(s)
    s = jnp.concatenate(s_rows, axis=0)
    m = jnp.max(s, axis=-1, keepdims=True)
    p = jnp.exp(s - m)
    p = p / jnp.sum(p, axis=-1, keepdims=True)
    for g in range(G):
        outs = []
        for h in range(H_X):
            r0 = (g * H_X + h) * lq
            ph = p[r0:r0 + lq, :].astype(BF16)
            for lb in range(nlb):
                vh = v_ref[g, pl.ds(lb * H_X + h, M, stride=stride), :].astype(BF16)
                outs.append(jnp.dot(ph, vh, preferred_element_type=F32))
        o_ref[g] = jnp.concatenate(outs, axis=-1).astype(o_ref.dtype)


def _xattn_cached(q, k_rows, v_rows, *, M, G):
    bsz, lq, d = q.shape
    kv = pl.BlockSpec((G, k_rows.shape[1], LANES), lambda b: (b, 0, 0))
    return pl.pallas_call(
        functools.partial(_xattn_cached_body, dh=d // H_X, M=M, G=G),
        grid=(bsz // G,),
        in_specs=[pl.BlockSpec((G, lq, d), lambda b: (b, 0, 0)), kv, kv],
        out_specs=pl.BlockSpec((G, lq, d), lambda b: (b, 0, 0)),
        out_shape=jax.ShapeDtypeStruct((bsz, lq, d), BF16),
        compiler_params=_cparams(("parallel",)),
        name="xattn_cached",
    )(q, k_rows, v_rows)


def _resproj_body(x_ref, a_ref, w_ref, o_ref):
    o_ref[...] = x_ref[...] + jnp.dot(a_ref[...], w_ref[...], preferred_element_type=F32)


def _resproj(x, a, w, *, tm):
    n, d = x.shape
    row = lambda wd: pl.BlockSpec((tm, wd), lambda i: (i, 0))
    return pl.pallas_call(
        _resproj_body,
        grid=(n // tm,),
        in_specs=[row(d), row(a.shape[1]), pl.BlockSpec(w.shape, lambda i: (0, 0))],
        out_specs=row(d),
        out_shape=jax.ShapeDtypeStruct((n, d), F32),
        compiler_params=_cparams(("parallel",)),
        name="resproj",
    )(x, a, w)


def kernel(x_prompt, x_sample, mem_prompt, cache_diff_k, cache_diff_v, state_gla, cache_mem_k, cache_mem_v, page_table, rel_bias, ffn1_norm, ffn1_w_gate, ffn1_w_up, ffn1_w_down, mix_norm, w_in, gla_w_gate_up, gla_b_gate, gla_out_norm, diff_q_norm, diff_k_norm, diff_lam_q1, diff_lam_k1, diff_lam_q2, diff_lam_k2, diff_subln, w_out, xattn_norm, mem_norm, xattn_wq, xattn_wk, xattn_wv, xattn_q_norm, xattn_k_norm, xattn_wo, ffn2_norm, ffn2_w_gate, ffn2_w_up, ffn2_w_down):
    depth = ffn1_norm.shape[0]
    B, L, D = x_prompt.shape
    Bd, Ld, _ = x_sample.shape
    M = mem_prompt.shape[1]
    n_pool = cache_diff_k.shape[1]
    T = 512
    TM = 512
    P_PAGES = 32

    xp = x_prompt.reshape(B * L, D)
    xs = x_sample.reshape(Bd * Ld, D)
    outs = {k: [] for k in ("kp", "vp", "ks", "vs", "sp", "ss", "mk", "mv")}

    g64 = jnp.asarray(np.kron(np.eye(512 // DK_D, dtype=np.float32), np.ones((DK_D, DK_D), np.float32)), BF16)

    for l in range(depth):
        lam_init = 0.8 - 0.6 * math.exp(-0.3 * l)
        row = lambda a: a[l].reshape(1, -1)

        w1g, w1u, w1d = (w[l].astype(BF16) for w in (ffn1_w_gate, ffn1_w_up, ffn1_w_down))
        w2g, w2u, w2d = (w[l].astype(BF16) for w in (ffn2_w_gate, ffn2_w_up, ffn2_w_down))
        wi = w_in[l]
        lr0 = 2 * H_G * DK_G + 2 * H_G * DV_G
        w_main = jnp.concatenate([wi[:, :lr0], wi[:, lr0 + GLA_RANK:]], axis=1).astype(BF16)
        w_lr = jnp.pad(wi[:, lr0:lr0 + GLA_RANK], ((0, 0), (0, 128 - GLA_RANK))).astype(BF16)
        w_up = jnp.pad(gla_w_gate_up[l], ((0, 128 - GLA_RANK), (0, 0))).astype(BF16)
        qn = jnp.tile(diff_q_norm[l], 2 * H_D).reshape(1, -1)
        kn = jnp.tile(diff_k_norm[l], 2 * H_D).reshape(1, -1)
        lam4 = jnp.stack([diff_lam_q1[l], diff_lam_k1[l], diff_lam_q2[l], diff_lam_k2[l]]).astype(F32)
        wo_mix = w_out[l].astype(BF16)
        wq_x, wk_x, wv_x, wo_x = (w[l].astype(BF16) for w in (xattn_wq, xattn_wk, xattn_wv, xattn_wo))

        tab = rel_bias.astype(F32).T.reshape(-1)
        bias_tiles = _bias_prompt_tiles(tab, T=T)
        bias_paged = _bias_paged_tiles(tab, LQ=Ld)

        xp = _ffn(xp, row(ffn1_norm), w1g, w1u, w1d, tm=1024, tf=256)
        xs = _ffn(xs, row(ffn1_norm), w1g, w1u, w1d, tm=1024, tf=256)

        mix_args = (row(mix_norm), w_main, w_lr, w_up, row(gla_b_gate), qn, kn, g64)
        qg, kg, vg, rg, lf, qd, kdt, kdb, vd, vdb = _mixin(xp, *mix_args, tm=TM, seq=L)
        qg_s, kg_s, vg_s, rg_s, lf_s, qd_s, kd_s, kdb_s, vd_s, vdb_s = _mixin(xs, *mix_args, tm=TM)
        b3 = lambda a: a.reshape(B, L, -1)
        s3 = lambda a: a.reshape(Bd, Ld, -1)
        og_s, ss = _gla(s3(qg_s), s3(kg_s), s3(vg_s), s3(lf_s), state_gla[l], tb=Ld, G=8)

        kt_pool = jnp.transpose(cache_diff_k[l], (0, 2, 3, 4, 1)).reshape(n_pool, H_D * 2 * DK_D, PAGE)
        v_pool = cache_diff_v[l].reshape(n_pool, PAGE * H_D, DV_D)
        s0 = jnp.zeros((B, H_G, DK_G, DV_G), F32)
        od_s, og, sp = _paged_attn_with_gla(s3(qd_s), s3(kdb_s), s3(vdb_s), kt_pool, v_pool, page_table,
                                            bias_paged, b3(qg), b3(kg), b3(vg), b3(lf), s0, P=P_PAGES)

        od = _dattn_prompt(b3(qd), b3(kdb), b3(vdb), bias_tiles, T=T)
        dh = D // H_X
        mem2 = mem_prompt.reshape(B * M, D)
        mk, mkb = _proj(mem2, row(mem_norm), wk_x, row(xattn_k_norm), tm=TM, head_norm=True, scale=1.0,
                        out_dtypes=(F32, BF16))
        mv, mvb = _proj(mem2, row(mem_norm), wv_x, jnp.ones((1, dh), F32), tm=TM, head_norm=False,
                        scale=1.0, out_dtypes=(F32, BF16))
        xp = _mix_xattn(xp, og.reshape(B * L, -1), rg, od.reshape(B * L, -1), lam4, row(gla_out_norm),
                        row(diff_subln), wo_mix, row(xattn_norm), wq_x, row(xattn_q_norm),
                        mkb.reshape(B, M, D), mvb.reshape(B, M, D), wo_x, tm=TM, seq=L, lam_init=lam_init)
        outs["mk"].append(mk.reshape(B, M, H_X, dh))
        outs["mv"].append(mv.reshape(B, M, H_X, dh))
        outs["kp"].append(kdt.reshape(B, H_D, 2, DK_D, L).transpose(0, 4, 1, 2, 3))
        outs["vp"].append(vd.reshape(B, L, H_D, DV_D))
        outs["sp"].append(sp)

        od_s = od_s.reshape(Bd, H_D, 2, Ld, DV_D).transpose(0, 3, 1, 2, 4).reshape(Bd * Ld, -1)
        xs = _mixout(xs, og_s.reshape(Bd * Ld, -1), rg_s, od_s, lam4, row(gla_out_norm), row(diff_subln),
                     wo_mix, tm=TM, lam_init=lam_init)
        outs["ks"].append(kd_s.reshape(Bd, Ld, H_D, 2, DK_D))
        outs["vs"].append(vd_s.reshape(Bd, Ld, H_D, DV_D))
        outs["ss"].append(ss)

        nlb = dh // LANES
        (qs,) = _proj(xs, row(xattn_norm), wq_x, row(xattn_q_norm), tm=TM, head_norm=True,
                      scale=dh ** -0.5, out_dtypes=(BF16,))
        cache_rows = lambda c: (c[l].reshape(Bd, M, H_X, nlb, LANES).transpose(0, 1, 3, 2, 4)
                                .reshape(Bd, M * nlb * H_X, LANES))
        osm = _xattn_cached(qs.reshape(Bd, Ld, D), cache_rows(cache_mem_k), cache_rows(cache_mem_v), M=M,
                            G=8)
        xs = _resproj(xs, osm.reshape(Bd * Ld, D), wo_x, tm=TM)

        xp = _ffn(xp, row(ffn2_norm), w2g, w2u, w2d, tm=1024, tf=256)
        xs = _ffn(xs, row(ffn2_norm), w2g, w2u, w2d, tm=1024, tf=256)

    st = lambda key: jnp.stack(outs[key])
    return (xp.reshape(B, L, D), xs.reshape(Bd, Ld, D), st("kp"), st("vp"), st("ks"), st("vs"),
            st("sp"), st("ss"), st("mk"), st("mv"))
```

```python
import functools
import math

import numpy as np
import jax
import jax.numpy as jnp
from jax import lax
from jax.experimental import pallas as pl
from jax.experimental.pallas import tpu as pltpu

F32 = jnp.float32
BF16 = jnp.bfloat16

EPS = 1e-6
NEG = -1e30
LOG2E = 1.0 / math.log(2.0)

H_G, DK_G, DV_G = 4, 64, 128
GLA_RANK = 16
GLA_GATE_NORM = 16.0
GLA_CHUNK = 16
H_D, DK_D, DV_D = 4, 64, 128
N_BUCKETS = 32
T5_MAX_EXACT = N_BUCKETS // 2
T5_MAX_DIST = 128
H_X = 4
PAGE = 128
LANES = 128

VMEM_LIMIT = 56 * 1024 * 1024


def _cparams(sem):
    return pltpu.CompilerParams(dimension_semantics=sem, vmem_limit_bytes=VMEM_LIMIT)


def _rms_full(x, g):
    ms = jnp.mean(x * x, axis=-1, keepdims=True)
    return x * lax.rsqrt(ms + EPS) * g


def _idiv(x, d):
    if d & (d - 1) == 0:
        return lax.shift_right_logical(x, int(math.log2(d)))
    return x // d


def _nt_dot(a, b):
    return lax.dot_general(a, b, (((1,), (1,)), ((), ())), preferred_element_type=F32)


def _ffn_body(x_ref, g_ref, wg_ref, wu_ref, wd_ref, o_ref, h_scr, acc_scr, *, tf):
    h_scr[...] = _rms_full(x_ref[...], g_ref[...]).astype(BF16)
    acc_scr[...] = jnp.zeros_like(acc_scr)

    def chunk(j, carry):
        c0 = pl.multiple_of(j * tf, tf)
        h = h_scr[...]
        gate = jnp.dot(h, wg_ref[:, pl.ds(c0, tf)], preferred_element_type=F32)
        up = jnp.dot(h, wu_ref[:, pl.ds(c0, tf)], preferred_element_type=F32)
        a = (gate * jax.nn.sigmoid(gate) * up).astype(BF16)
        acc_scr[...] += jnp.dot(a, wd_ref[pl.ds(c0, tf), :], preferred_element_type=F32)
        return carry

    lax.fori_loop(0, wg_ref.shape[1] // tf, chunk, 0)
    o_ref[...] = x_ref[...] + 0.5 * acc_scr[...]


def _ffn(x, g, wg, wu, wd, *, tm, tf):
    n, d = x.shape
    resident = lambda shape: pl.BlockSpec(shape, lambda i: (0, 0), pipeline_mode=pl.Buffered(1))
    return pl.pallas_call(
        functools.partial(_ffn_body, tf=tf),
        grid=(n // tm,),
        in_specs=[pl.BlockSpec((tm, d), lambda i: (i, 0)), resident((1, d)),
                  resident(wg.shape), resident(wu.shape), resident(wd.shape)],
        out_specs=pl.BlockSpec((tm, d), lambda i: (i, 0)),
        out_shape=jax.ShapeDtypeStruct((n, d), F32),
        scratch_shapes=[pltpu.VMEM((tm, d), BF16), pltpu.VMEM((tm, d), F32)],
        compiler_params=_cparams(("parallel",)),
        name="ffn",
    )(x, g, wg, wu, wd)


def _mixin_body(x_ref, g_ref, w_ref, wlr_ref, wup_ref, bg_ref, qn_ref, kn_ref, g64_ref,
                qg_ref, kg_ref, vg_ref, rg_ref, lf_ref, qd_ref, kd_ref, kdb_ref, vd_ref, vdb_ref,
                *, k_transposed):
    h = _rms_full(x_ref[...], g_ref[...]).astype(BF16)
    y = jnp.dot(h, w_ref[...], preferred_element_type=F32)
    qg_ref[...] = y[:, 0:256] * (DK_G ** -0.5)
    kg_ref[...] = y[:, 256:512]
    vg_ref[...] = y[:, 512:1024]
    rg_ref[...] = y[:, 1024:1536]
    qd = y[:, 1536:2048]
    kd = y[:, 2048:2560]
    vd = y[:, 2560:3072]

    glr = jnp.dot(h, wlr_ref[...], preferred_element_type=F32)
    z = jnp.dot(glr.astype(BF16), wup_ref[...], preferred_element_type=F32) + bg_ref[...]
    lf_ref[...] = (jnp.minimum(z, 0.0) - jnp.log(1.0 + jnp.exp(-jnp.abs(z)))) * (1.0 / GLA_GATE_NORM)

    def group_rms(t, gain):
        msq = jnp.dot((t * t).astype(BF16), g64_ref[...], preferred_element_type=F32) * (1.0 / DK_D)
        return t * lax.rsqrt(msq + EPS) * gain

    qdn = group_rms(qd, qn_ref[...]) * (DK_D ** -0.5 * LOG2E)
    kdn = group_rms(kd, kn_ref[...])
    qd_ref[...] = qdn.astype(BF16)
    if k_transposed:
        kd_ref[0] = kdn.T
    else:
        kd_ref[...] = kdn
    kdb_ref[...] = kdn.astype(BF16)
    for h in range(H_D):
        vd_ref[pl.ds(h, vd.shape[0], stride=H_D), :] = vd[:, h * DV_D:(h + 1) * DV_D]
    vdb_ref[...] = vd.astype(BF16)


def _mixin(x, g, w_main, w_lr, w_up, b_gate, qn, kn, g64, *, tm, seq=None):
    n, d = x.shape
    full = lambda shape: pl.BlockSpec(shape, lambda i: (0, 0))
    row = lambda w: pl.BlockSpec((tm, w), lambda i: (i, 0))
    outs = [(256, F32), (256, F32), (512, F32), (512, F32), (256, F32),
            (512, BF16), (512, F32), (512, BF16), (512, F32), (512, BF16)]
    out_specs = [row(w) for w, _ in outs]
    out_shape = [jax.ShapeDtypeStruct((n, w), dt) for w, dt in outs]
    if seq is not None:
        nt = seq // tm
        out_specs[6] = pl.BlockSpec((1, 512, tm), lambda i: (i // nt, 0, i % nt))
        out_shape[6] = jax.ShapeDtypeStruct((n // seq, 512, seq), F32)
    out_specs[8] = pl.BlockSpec((tm * H_D, DV_D), lambda i: (i, 0))
    out_shape[8] = jax.ShapeDtypeStruct((n * H_D, DV_D), F32)
    return pl.pallas_call(
        functools.partial(_mixin_body, k_transposed=seq is not None),
        grid=(n // tm,),
        in_specs=[row(d), full((1, d)), full(w_main.shape), full(w_lr.shape), full(w_up.shape),
                  full((1, 256)), full((1, 512)), full((1, 512)), full((512, 512))],
        out_specs=out_specs,
        out_shape=out_shape,
        compiler_params=_cparams(("parallel",)),
        name="mixer_in",
    )(x, g, w_main, w_lr, w_up, b_gate, qn, kn, g64)


def _gla_phases(tb, n_tb, q_ref, k_ref, v_ref, lf_ref, s0_ref, tri_ref, g2_ref, o_ref, so_ref,
                st_scr, nat_scr, b_scr, *, C, n_chunks, unroll):
    def init():
        @pl.when(tb == 0)
        def _():
            nat_scr[...] = jnp.zeros_like(nat_scr)
            for h in range(H_G):
                nat_scr[h * DK_G:(h + 1) * DK_G, h * DV_G:(h + 1) * DV_G] = s0_ref[0, h]
            st_scr[...] = nat_scr[...].T

    def final():
        @pl.when(tb == n_tb - 1)
        def _():
            nat = st_scr[...].T
            for h in range(H_G):
                so_ref[0, h] = nat[h * DK_G:(h + 1) * DK_G, h * DV_G:(h + 1) * DV_G]

    masks = {}

    def chunk(c, carry):
        head_mask = masks["head"]
        t_idx = masks["t"]
        r0 = pl.multiple_of(c * C, C)
        q = q_ref[0, pl.ds(r0, C), :]
        k = k_ref[0, pl.ds(r0, C), :]
        v = v_ref[0, pl.ds(r0, C), :]
        b = b_scr[pl.ds(r0, C), :]
        st = st_scr[...]

        o = _nt_dot((q * jnp.exp(b)).astype(BF16), st.astype(BF16))

        pieces = []
        for s in range(C):
            rel = jnp.where(t_idx >= s, b - b[s:s + 1, :], NEG)
            pieces.append(q * k[s:s + 1, :] * jnp.exp(rel))
        w = jnp.concatenate(pieces, axis=0).astype(BF16)
        a = jnp.dot(w, g2_ref[...], preferred_element_type=F32)
        for s in range(C):
            o = o + a[s * C:(s + 1) * C, :] * v[s:s + 1, :]
        o_ref[0, pl.ds(r0, C), :] = o

        b_last = b[C - 1:C, :]
        kt = (k * jnp.exp(b_last - b)).astype(BF16)
        upd = lax.dot_general(v.astype(BF16), kt, (((0,), (0,)), ((), ())),
                              preferred_element_type=F32)
        st_scr[...] = st * jnp.exp(b_last) + jnp.where(head_mask, upd, 0.0)
        return carry

    def main():
        rows = lax.broadcasted_iota(jnp.int32, (H_G * DV_G, H_G * DK_G), 0)
        cols = lax.broadcasted_iota(jnp.int32, (H_G * DV_G, H_G * DK_G), 1)
        masks["head"] = _idiv(rows, DV_G) == _idiv(cols, DK_G)
        masks["t"] = lax.broadcasted_iota(jnp.int32, (C, H_G * DK_G), 0)
        b_scr[...] = jnp.dot(tri_ref[...], lf_ref[0], preferred_element_type=F32,
                             precision=lax.Precision.HIGHEST)
        lax.fori_loop(0, n_chunks, chunk, 0, unroll=unroll)

    return init, main, final


def _gla_body(*refs, C, n_chunks, G):
    q_ref, k_ref, v_ref, lf_ref, s0_ref, tri_ref, g2_ref, o_ref, so_ref = refs[:9]
    scratch = refs[9:]
    per_seq = []
    for g in range(G):
        one = lambda r, g=g: r.at[pl.ds(g, 1)]
        per_seq.append(_gla_phases(
            pl.program_id(1), pl.num_programs(1), one(q_ref), one(k_ref), one(v_ref), one(lf_ref),
            one(s0_ref), tri_ref, g2_ref, one(o_ref), one(so_ref), *[s.at[g] for s in scratch],
            C=C, n_chunks=n_chunks, unroll=min(2, n_chunks)))
    for phases in zip(*per_seq):
        for phase in phases:
            phase()


def _gla_consts(tb, C):
    tri = np.kron(np.eye(tb // C, dtype=np.float32), np.tril(np.ones((C, C), np.float32)))
    g2 = np.kron(np.eye(H_G, dtype=np.float32), np.ones((DK_G, DV_G), np.float32))
    return jnp.asarray(tri), jnp.asarray(g2, BF16)


def _gla_scratch(tb, lead=()):
    return [pltpu.VMEM(lead + (H_G * DV_G, H_G * DK_G), F32), pltpu.VMEM(lead + (H_G * DK_G, H_G * DV_G), F32),
            pltpu.VMEM(lead + (tb, H_G * DK_G), F32)]


def _gla(q, k, v, lf, s0, *, tb, G):
    bsz, seq, _ = q.shape
    C = math.gcd(seq, GLA_CHUNK)
    tb = min(tb, seq)
    tri, g2 = _gla_consts(tb, C)
    tok = lambda w: pl.BlockSpec((G, tb, w), lambda b, t: (b, t, 0))
    st_spec = pl.BlockSpec((G, H_G, DK_G, DV_G), lambda b, t: (b, 0, 0, 0))
    return pl.pallas_call(
        functools.partial(_gla_body, C=C, n_chunks=tb // C, G=G),
        grid=(bsz // G, seq // tb),
        in_specs=[tok(256), tok(256), tok(512), tok(256), st_spec,
                  pl.BlockSpec((tb, tb), lambda b, t: (0, 0)),
                  pl.BlockSpec((256, 512), lambda b, t: (0, 0))],
        out_specs=[tok(512), st_spec],
        out_shape=[jax.ShapeDtypeStruct((bsz, seq, 512), F32),
                   jax.ShapeDtypeStruct((bsz, H_G, DK_G, DV_G), F32)],
        scratch_shapes=_gla_scratch(tb, (G,)),
        compiler_params=_cparams(("parallel", "arbitrary")),
        name="gla",
    )(q, k, v, lf, s0, tri, g2)


def _t5_shifted_bias(n, tab_ref, h):
    nf = jnp.maximum(n, 1).astype(F32)
    large = T5_MAX_EXACT + (jnp.log(nf / T5_MAX_EXACT) / math.log(T5_MAX_DIST / T5_MAX_EXACT)
                            * (N_BUCKETS - T5_MAX_EXACT)).astype(jnp.int32)
    large = jnp.minimum(large, N_BUCKETS - 1)
    bucket = jnp.where(n < T5_MAX_EXACT, n, large)
    far = tab_ref[h * N_BUCKETS + N_BUCKETS - 1]
    val = jnp.zeros(n.shape, F32)
    for b in range(N_BUCKETS - 1):
        val = jnp.where(bucket == b, (tab_ref[h * N_BUCKETS + b] - far) * LOG2E, val)
    return val


def _bias_prompt_body(tab_ref, o_ref, *, T, RB):
    h = pl.program_id(0)
    t = pl.program_id(1)
    for r0 in range(0, T, RB):
        blocks = [c0 for c0 in range(0, T, LANES)
                  if (r0 - (c0 + LANES - 1) < T5_MAX_DIST and r0 + RB - 1 - c0 >= 0)
                  or T + r0 - (c0 + LANES - 1) < T5_MAX_DIST]
        for c0 in range(0, T, LANES):
            r = lax.broadcasted_iota(jnp.int32, (RB, LANES), 0) + r0
            c = lax.broadcasted_iota(jnp.int32, (RB, LANES), 1) + c0
            n = r - c + t * T
            if c0 in blocks:
                val = _t5_shifted_bias(jnp.maximum(n, 0), tab_ref, h)
            else:
                val = jnp.zeros((RB, LANES), F32)
            o_ref[0, 0, r0:r0 + RB, c0:c0 + LANES] = jnp.where(n >= 0, val, NEG)


def _bias_prompt_tiles(tab, *, T):
    assert T >= T5_MAX_DIST
    return pl.pallas_call(
        functools.partial(_bias_prompt_body, T=T, RB=64),
        grid=(H_D, 2),
        in_specs=[pl.BlockSpec(memory_space=pltpu.SMEM)],
        out_specs=pl.BlockSpec((1, 1, T, T), lambda h, t: (h, t, 0, 0)),
        out_shape=jax.ShapeDtypeStruct((H_D, 2, T, T), F32),
        compiler_params=_cparams(("parallel", "parallel")),
        name="bias_prompt",
    )(tab)


def _bias_paged_body(tab_ref, o_ref, *, LQ):
    t = pl.program_id(0)
    rh = 2 * LQ
    r = lax.broadcasted_iota(jnp.int32, (rh, PAGE), 0)
    c = lax.broadcasted_iota(jnp.int32, (rh, PAGE), 1)
    n = (r - _idiv(r, LQ) * LQ) - c + (1 - t) * PAGE
    for h in range(H_D):
        val = _t5_shifted_bias(jnp.maximum(n, 0), tab_ref, h)
        o_ref[0, h * rh:(h + 1) * rh, :] = jnp.where(n >= 0, val, NEG)


def _bias_paged_tiles(tab, *, LQ):
    assert PAGE >= T5_MAX_DIST
    R = H_D * 2 * LQ
    return pl.pallas_call(
        functools.partial(_bias_paged_body, LQ=LQ),
        grid=(2,),
        in_specs=[pl.BlockSpec(memory_space=pltpu.SMEM)],
        out_specs=pl.BlockSpec((1, R, PAGE), lambda t: (t, 0, 0)),
        out_shape=jax.ShapeDtypeStruct((2, R, PAGE), F32),
        compiler_params=_cparams(("parallel",)),
        name="bias_paged",
    )(tab)


def _dattn_body(q_ref, k_ref, v_ref, bias_ref, o_ref, qm_scr, m_scr, l_scr, acc_scr, *, T):
    qi = pl.program_id(2)
    m_scr[...] = jnp.full_like(m_scr, NEG)
    l_scr[...] = jnp.zeros_like(l_scr)
    acc_scr[...] = jnp.zeros_like(acc_scr)
    q = q_ref[0]
    lane = lax.broadcasted_iota(jnp.int32, q.shape, 1)
    qm_scr[0] = jnp.where(lane < DK_D, q, jnp.zeros_like(q))
    qm_scr[1] = jnp.where(lane >= DK_D, q, jnp.zeros_like(q))

    def step(kj, r0, nr, bias):
        c0 = pl.multiple_of(kj * T, T)
        k = k_ref[0, pl.ds(c0, T), :]
        v = v_ref[0, pl.ds(c0, T), :]
        for mp in range(2):
            s = _nt_dot(qm_scr[mp, r0:r0 + nr, :], k)
            if bias is not None:
                nb = bias.shape[0]
                s = s + bias if nb == nr else jnp.concatenate([s[:nb] + bias, s[nb:]], axis=0)
            m_prev = m_scr[mp, r0:r0 + nr, :]
            m_new = jnp.maximum(m_prev, jnp.max(s, axis=-1, keepdims=True))
            corr = jnp.exp2(m_prev - m_new)
            p = jnp.exp2((s - jnp.concatenate([m_new] * (T // LANES), axis=1)).astype(BF16))
            l_scr[mp, r0:r0 + nr, :] = (l_scr[mp, r0:r0 + nr, :] * corr
                                        + jnp.sum(p.astype(F32), axis=-1, keepdims=True))
            acc_scr[mp, r0:r0 + nr, :] = (acc_scr[mp, r0:r0 + nr, :] * corr
                                          + jnp.dot(p, v, preferred_element_type=F32))
            m_scr[mp, r0:r0 + nr, :] = m_new

    def far_step(kj, carry):
        step(kj, 0, 2 * T, None)
        return carry

    lax.fori_loop(0, jnp.maximum(2 * qi - 1, 0), far_step, 0)

    diag = bias_ref[0, 0]
    sub = bias_ref[0, 1]

    @pl.when(qi >= 1)
    def _():
        step(2 * qi - 1, 0, 2 * T, sub)

    step(2 * qi, 0, 2 * T, jnp.concatenate([diag, sub], axis=0))
    step(2 * qi + 1, T, T, diag)
    for mp in range(2):
        o_ref[0, :, mp * DV_D:(mp + 1) * DV_D] = acc_scr[mp] / l_scr[mp]


def _dattn_prompt(q, k, v, bias_tiles, *, T):
    bsz, seq, _ = q.shape
    tq = 2 * T
    kv_spec = pl.BlockSpec((1, seq, 128), lambda b, h, qi: (b, 0, h))
    return pl.pallas_call(
        functools.partial(_dattn_body, T=T),
        grid=(bsz, H_D, seq // tq),
        in_specs=[pl.BlockSpec((1, tq, 128), lambda b, h, qi: (b, qi, h)), kv_spec, kv_spec,
                  pl.BlockSpec((1, 2, T, T), lambda b, h, qi: (h, 0, 0, 0))],
        out_specs=pl.BlockSpec((1, tq, 2 * DV_D), lambda b, h, qi: (b, qi, h)),
        out_shape=jax.ShapeDtypeStruct((bsz, seq, H_D * 2 * DV_D), F32),
        scratch_shapes=[pltpu.VMEM((2, tq, 128), BF16), pltpu.VMEM((2, tq, LANES), F32),
                        pltpu.VMEM((2, tq, LANES), F32), pltpu.VMEM((2, tq, DV_D), F32)],
        compiler_params=_cparams(("parallel", "parallel", "arbitrary")),
        name="diff_attn_prompt",
    )(q, k, v, bias_tiles)


def _paged_phases(j, n_j, q_ref, kn_ref, vn_ref, bias_ref, k_refs, v_refs, o_ref,
                  wq_scr, m_scr, l_scr, acc_scr, *, LQ):
    P = len(k_refs)
    RH = 2 * LQ
    is_last = j == n_j - 1

    def update(s, v_heads):
        n = s.shape[1] // PAGE
        m_prev = m_scr[...]
        m_new = jnp.maximum(m_prev, jnp.max(s, axis=-1, keepdims=True))
        corr = jnp.exp2(m_prev - m_new)
        p = jnp.exp2(s - jnp.concatenate([m_new] * n, axis=1))
        l_scr[...] = l_scr[...] * corr + jnp.sum(p, axis=-1, keepdims=True)
        pb = p.astype(BF16)
        pv = [jnp.dot(pb[h * RH:(h + 1) * RH, :], v_heads[h], preferred_element_type=F32)
              for h in range(H_D)]
        acc_scr[...] = acc_scr[...] * corr + jnp.concatenate(pv, axis=0)
        m_scr[...] = m_new

    def init():
        @pl.when(j == 0)
        def _():
            q = q_ref[0]
            qt = jnp.concatenate([q] * (H_D * 2), axis=0)
            rr = lax.broadcasted_iota(jnp.int32, qt.shape, 0)
            cc = lax.broadcasted_iota(jnp.int32, qt.shape, 1)
            wq_scr[...] = jnp.where(_idiv(rr, LQ) == _idiv(cc, DK_D), qt, jnp.zeros_like(qt))
            m_scr[...] = jnp.full_like(m_scr, NEG)
            l_scr[...] = jnp.zeros_like(l_scr)
            acc_scr[...] = jnp.zeros_like(acc_scr)
            pad = jnp.zeros((PAGE - LQ, H_D * 2 * DK_D), BF16)
            kn = jnp.concatenate([kn_ref[0], pad], axis=0)
            vn = jnp.concatenate([vn_ref[0], pad], axis=0)
            update(_nt_dot(wq_scr[...], kn) + bias_ref[1],
                   [vn[:, h * DV_D:(h + 1) * DV_D] for h in range(H_D)])

    def main():
        kt = jnp.concatenate([k_refs[i][...].astype(BF16) for i in range(P)], axis=1)
        s = jnp.dot(wq_scr[...], kt, preferred_element_type=F32)
        last_bias = bias_ref[0] * jnp.where(is_last, 1.0, 0.0)
        s = jnp.concatenate([s[:, :(P - 1) * PAGE], s[:, (P - 1) * PAGE:] + last_bias], axis=1)
        v_heads = [jnp.concatenate([v_refs[i][pl.ds(h, PAGE, stride=H_D), :].astype(BF16)
                                    for i in range(P)], axis=0) for h in range(H_D)]
        update(s, v_heads)

    def final():
        @pl.when(is_last)
        def _():
            o_ref[0] = acc_scr[...] / l_scr[...]

    return init, main, final


def _paged_gla_body(pt_ref, q_ref, kn_ref, vn_ref, bias_ref, kpool_ref, vpool_ref, *refs,
                    P, LQ, n_j, n_tb, C, n_chunks):
    gla_in = refs[:7]
    o_ref, go_ref, gs_ref = refs[7:10]
    paged_scr = refs[10:14]
    gla_scr = refs[14:17]
    kbuf, vbuf, sem = refs[17:20]
    s = pl.program_id(0)
    steps = pl.num_programs(0)
    slot = lax.rem(s, 2)

    def page_copies(step, slot_, known_pages):
        copies = []
        for i in range(P):
            page = pt_ref[step * P + i] if known_pages else 0
            copies.append(pltpu.make_async_copy(kpool_ref.at[page], kbuf.at[slot_, i], sem.at[slot_, 0]))
            copies.append(pltpu.make_async_copy(vpool_ref.at[page], vbuf.at[slot_, i], sem.at[slot_, 1]))
        return copies

    @pl.when(s == 0)
    def _():
        for c in page_copies(0, 0, True):
            c.start()

    gla = _gla_phases(lax.rem(s, n_tb), n_tb, *gla_in, go_ref, gs_ref, *gla_scr, C=C, n_chunks=n_chunks,
                      unroll=n_chunks)
    paged = _paged_phases(lax.rem(s, n_j), n_j, q_ref, kn_ref, vn_ref, bias_ref,
                          [kbuf.at[slot, i] for i in range(P)], [vbuf.at[slot, i] for i in range(P)],
                          o_ref, *paged_scr, LQ=LQ)
    gla[0]()
    paged[0]()
    for c in page_copies(lax.rem(s + 1, steps), 1 - slot, True):
        c.start()
    gla[1]()
    for c in page_copies(s, slot, False):
        c.wait()
    paged[1]()
    gla[2]()
    paged[2]()

    @pl.when(s == steps - 1)
    def _():
        for c in page_copies(0, 1 - slot, False):
            c.wait()


def _paged_attn_with_gla(q, k_new, v_new, kt_pool, v_pool, page_table, bias_tiles,
                         gq, gk, gv, glf, gs0, *, P):
    bd, lq, _ = q.shape
    n_pages = page_table.shape[1]
    n_j = n_pages // P
    steps = bd * n_j
    bsz, seq, _ = gq.shape
    C = math.gcd(seq, GLA_CHUNK)
    tb = bsz * seq // steps
    assert tb * steps == bsz * seq and tb % C == 0 and seq % tb == 0
    n_tb = seq // tb
    tri, g2 = _gla_consts(tb, C)
    R = H_D * 2 * lq
    assert n_pages == n_j * P
    tok = pl.BlockSpec((1, lq, 512), lambda s, pt: (s // n_j, 0, 0))
    const = lambda shape: pl.BlockSpec(shape, lambda s, pt: (0,) * len(shape))
    pool = pl.BlockSpec(memory_space=pl.ANY)
    gtok = lambda w: pl.BlockSpec((1, tb, w), lambda s, pt: (s // n_tb, s % n_tb, 0))
    gst = pl.BlockSpec((1, H_G, DK_G, DV_G), lambda s, pt: (s // n_tb, 0, 0, 0))
    page_buf = pltpu.VMEM((2, P) + kt_pool.shape[1:], F32)
    grid_spec = pltpu.PrefetchScalarGridSpec(
        num_scalar_prefetch=1,
        grid=(steps,),
        in_specs=[tok, tok, tok, const((2, R, PAGE)), pool, pool,
                  gtok(256), gtok(256), gtok(512), gtok(256), gst, const((tb, tb)), const((256, 512))],
        out_specs=[pl.BlockSpec((1, R, DV_D), lambda s, pt: (s // n_j, 0, 0)), gtok(512), gst],
        scratch_shapes=[pltpu.VMEM((R, 512), BF16), pltpu.VMEM((R, LANES), F32),
                        pltpu.VMEM((R, LANES), F32), pltpu.VMEM((R, DV_D), F32)] + _gla_scratch(tb)
        + [page_buf, page_buf, pltpu.SemaphoreType.DMA((2, 2))],
    )
    return pl.pallas_call(
        functools.partial(_paged_gla_body, P=P, LQ=lq, n_j=n_j, n_tb=n_tb, C=C, n_chunks=tb // C),
        grid_spec=grid_spec,
        out_shape=[jax.ShapeDtypeStruct((bd, R, DV_D), F32),
                   jax.ShapeDtypeStruct((bsz, seq, 512), F32),
                   jax.ShapeDtypeStruct((bsz, H_G, DK_G, DV_G), F32)],
        compiler_params=_cparams(("arbitrary",)),
        name="paged_attn_gla",
    )(page_table.reshape(-1), q, k_new, v_new, bias_tiles, kt_pool, v_pool,
      gq, gk, gv, glf, gs0, tri, g2)


def _mixout_math(x, og, rg, od, lam4, gn, sn, wo, lam_init):
    lam = (jnp.exp(jnp.sum(lam4[0:1, :] * lam4[1:2, :], axis=-1, keepdims=True))
           - jnp.exp(jnp.sum(lam4[2:3, :] * lam4[3:4, :], axis=-1, keepdims=True)) + lam_init)
    parts = []
    for h in range(H_G):
        t = og[:, h * DV_G:(h + 1) * DV_G]
        r = rg[:, h * DV_G:(h + 1) * DV_G]
        parts.append(_rms_full(t, gn) * (r * jax.nn.sigmoid(r)))
    for h in range(H_D):
        d = od[:, h * 2 * DV_D:h * 2 * DV_D + DV_D] - lam * od[:, h * 2 * DV_D + DV_D:(h + 1) * 2 * DV_D]
        parts.append(_rms_full(d, sn) * (1.0 - lam_init))
    cat = jnp.concatenate(parts, axis=-1).astype(BF16)
    return x + jnp.dot(cat, wo, preferred_element_type=F32)


def _mixout_body(x_ref, og_ref, rg_ref, od_ref, lam4_ref, gn_ref, sn_ref, wo_ref, o_ref, *, lam_init):
    o_ref[...] = _mixout_math(x_ref[...], og_ref[...], rg_ref[...], od_ref[...], lam4_ref[...],
                              gn_ref[...], sn_ref[...], wo_ref[...], lam_init)


def _mix_xattn_body(x_ref, og_ref, rg_ref, od_ref, lam4_ref, gn_ref, sn_ref, wo_ref,
                    xg_ref, wq_ref, qn_ref, mk_ref, mv_ref, wox_ref, o_ref, *, lam_init, dh):
    x = _mixout_math(x_ref[...], og_ref[...], rg_ref[...], od_ref[...], lam4_ref[...],
                     gn_ref[...], sn_ref[...], wo_ref[...], lam_init)
    h = _rms_full(x, xg_ref[...]).astype(BF16)
    y = jnp.dot(h, wq_ref[...], preferred_element_type=F32)
    qn = qn_ref[...]
    k = mk_ref[0]
    v = mv_ref[0]
    outs = []
    for i in range(H_X):
        sl = slice(i * dh, (i + 1) * dh)
        q = (_rms_full(y[:, sl], qn) * (dh ** -0.5)).astype(BF16)
        s = _nt_dot(q, k[:, sl])
        p = jnp.exp(s - jnp.max(s, axis=-1, keepdims=True))
        p = p / jnp.sum(p, axis=-1, keepdims=True)
        outs.append(jnp.dot(p.astype(BF16), v[:, sl], preferred_element_type=F32))
    att = jnp.concatenate(outs, axis=-1).astype(BF16)
    o_ref[...] = x + jnp.dot(att, wox_ref[...], preferred_element_type=F32)


def _mix_xattn(x, og, rg, od, lam4, gn, sn, wo, xg, wq, qn, mk, mv, wox, *, tm, seq, lam_init):
    n, d = x.shape
    nt = seq // tm
    resident = lambda shape: pl.BlockSpec(shape, lambda i: (0,) * len(shape), pipeline_mode=pl.Buffered(1))
    row = lambda w: pl.BlockSpec((tm, w), lambda i: (i, 0))
    mem = pl.BlockSpec((1,) + mk.shape[1:], lambda i: (i // nt, 0, 0))
    return pl.pallas_call(
        functools.partial(_mix_xattn_body, lam_init=lam_init, dh=d // H_X),
        grid=(n // tm,),
        in_specs=[row(d), row(512), row(512), row(1024), resident((4, DK_D)), resident((1, DV_G)),
                  resident((1, DV_D)), resident(wo.shape), resident((1, d)), resident(wq.shape),
                  resident(qn.shape), mem, mem, resident(wox.shape)],
        out_specs=row(d),
        out_shape=jax.ShapeDtypeStruct((n, d), F32),
        compiler_params=_cparams(("parallel",)),
        name="mixer_out_xattn",
    )(x, og, rg, od, lam4, gn, sn, wo, xg, wq, qn, mk, mv, wox)


def _mixout(x, og, rg, od, lam4, gn, sn, wo, *, tm, lam_init):
    n, d = x.shape
    full = lambda shape: pl.BlockSpec(shape, lambda i: (0, 0))
    row = lambda w: pl.BlockSpec((tm, w), lambda i: (i, 0))
    return pl.pallas_call(
        functools.partial(_mixout_body, lam_init=lam_init),
        grid=(n // tm,),
        in_specs=[row(d), row(512), row(512), row(1024), full((4, DK_D)), full((1, DV_G)),
                  full((1, DV_D)), full(wo.shape)],
        out_specs=row(d),
        out_shape=jax.ShapeDtypeStruct((n, d), F32),
        compiler_params=_cparams(("parallel",)),
        name="mixer_out",
    )(x, og, rg, od, lam4, gn, sn, wo)


def _proj_body(x_ref, g_ref, w_ref, hn_ref, *o_refs, head_norm, scale, dh):
    h = _rms_full(x_ref[...], g_ref[...]).astype(BF16)
    y = jnp.dot(h, w_ref[...], preferred_element_type=F32)
    if head_norm:
        hn = hn_ref[...]
        y = jnp.concatenate(
            [_rms_full(y[:, i * dh:(i + 1) * dh], hn) for i in range(y.shape[1] // dh)], axis=-1)
    y = y * scale
    for o_ref in o_refs:
        o_ref[...] = y.astype(o_ref.dtype)


def _proj(x, g, w, hn, *, tm, head_norm, scale, out_dtypes):
    n, d = x.shape
    dout = w.shape[1]
    dh = hn.shape[1]
    full = lambda shape: pl.BlockSpec(shape, lambda i: (0, 0))
    row = lambda wd: pl.BlockSpec((tm, wd), lambda i: (i, 0))
    return pl.pallas_call(
        functools.partial(_proj_body, head_norm=head_norm, scale=scale, dh=dh),
        grid=(n // tm,),
        in_specs=[row(d), full((1, d)), full(w.shape), full((1, dh))],
        out_specs=[row(dout) for _ in out_dtypes],
        out_shape=[jax.ShapeDtypeStruct((n, dout), dt) for dt in out_dtypes],
        compiler_params=_cparams(("parallel",)),
        name="proj",
    )(x, g, w, hn)


def _xattn_cached_body(q_ref, k_ref, v_ref, o_ref, *, dh, M, G):
    nlb = dh // LANES
    stride = nlb * H_X
    lq = q_ref.shape[1]
    s_rows = []
    for g in range(G):
        q = q_ref[g]
        for h in range(H_X):
            s = None
            for lb in range(nlb):
                kh = k_ref[g, pl.ds(lb * H_X + h, M, stride=stride), :].astype(BF16)
                t = _nt_dot(q[:, h * dh + lb * LANES:h * dh + (lb + 1) * LANES], kh)
                s = t if s is None else s + t
            s_rows.append(s)
    s = jnp.concatenate(s_rows, axis=0)
    m = jnp.max(s, axis=-1, keepdims=True)
    p = jnp.exp(s - m)
    p = p / jnp.sum(p, axis=-1, keepdims=True)
    for g in range(G):
        outs = []
        for h in range(H_X):
            r0 = (g * H_X + h) * lq
            ph = p[r0:r0 + lq, :].astype(BF16)
            for lb in range(nlb):
                vh = v_ref[g, pl.ds(lb * H_X + h, M, stride=stride), :].astype(BF16)
                outs.append(jnp.dot(ph, vh, preferred_element_type=F32))
        o_ref[g] = jnp.concatenate(outs, axis=-1).astype(o_ref.dtype)


def _xattn_cached(q, k_rows, v_rows, *, M, G):
    bsz, lq, d = q.shape
    kv = pl.BlockSpec((G, k_rows.shape[1], LANES), lambda b: (b, 0, 0))
    return pl.pallas_call(
        functools.partial(_xattn_cached_body, dh=d // H_X, M=M, G=G),
        grid=(bsz // G,),
        in_specs=[pl.BlockSpec((G, lq, d), lambda b: (b, 0, 0)), kv, kv],
        out_specs=pl.BlockSpec((G, lq, d), lambda b: (b, 0, 0)),
        out_shape=jax.ShapeDtypeStruct((bsz, lq, d), BF16),
        compiler_params=_cparams(("parallel",)),
        name="xattn_cached",
    )(q, k_rows, v_rows)


def _resproj_body(x_ref, a_ref, w_ref, o_ref):
    o_ref[...] = x_ref[...] + jnp.dot(a_ref[...], w_ref[...], preferred_element_type=F32)


def _resproj(x, a, w, *, tm):
    n, d = x.shape
    row = lambda wd: pl.BlockSpec((tm, wd), lambda i: (i, 0))
    return pl.pallas_call(
        _resproj_body,
        grid=(n // tm,),
        in_specs=[row(d), row(a.shape[1]), pl.BlockSpec(w.shape, lambda i: (0, 0))],
        out_specs=row(d),
        out_shape=jax.ShapeDtypeStruct((n, d), F32),
        compiler_params=_cparams(("parallel",)),
        name="resproj",
    )(x, a, w)


def kernel(x_prompt, x_sample, mem_prompt, cache_diff_k, cache_diff_v, state_gla, cache_mem_k, cache_mem_v, page_table, rel_bias, ffn1_norm, ffn1_w_gate, ffn1_w_up, ffn1_w_down, mix_norm, w_in, gla_w_gate_up, gla_b_gate, gla_out_norm, diff_q_norm, diff_k_norm, diff_lam_q1, diff_lam_k1, diff_lam_q2, diff_lam_k2, diff_subln, w_out, xattn_norm, mem_norm, xattn_wq, xattn_wk, xattn_wv, xattn_q_norm, xattn_k_norm, xattn_wo, ffn2_norm, ffn2_w_gate, ffn2_w_up, ffn2_w_down):
    depth = ffn1_norm.shape[0]
    B, L, D = x_prompt.shape
    Bd, Ld, _ = x_sample.shape
    M = mem_prompt.shape[1]
    n_pool = cache_diff_k.shape[1]
    T = 512
    TM = 512
    P_PAGES = 32

    xp = x_prompt.reshape(B * L, D)
    xs = x_sample.reshape(Bd * Ld, D)
    outs = {k: [] for k in ("kp", "vp", "ks", "vs", "sp", "ss", "mk", "mv")}

    g64 = jnp.asarray(np.kron(np.eye(512 // DK_D, dtype=np.float32), np.ones((DK_D, DK_D), np.float32)), BF16)

    for l in range(depth):
        lam_init = 0.8 - 0.6 * math.exp(-0.3 * l)
        row = lambda a: a[l].reshape(1, -1)

        w1g, w1u, w1d = (w[l].astype(BF16) for w in (ffn1_w_gate, ffn1_w_up, ffn1_w_down))
        w2g, w2u, w2d = (w[l].astype(BF16) for w in (ffn2_w_gate, ffn2_w_up, ffn2_w_down))
        wi = w_in[l]
        lr0 = 2 * H_G * DK_G + 2 * H_G * DV_G
        w_main = jnp.concatenate([wi[:, :lr0], wi[:, lr0 + GLA_RANK:]], axis=1).astype(BF16)
        w_lr = jnp.pad(wi[:, lr0:lr0 + GLA_RANK], ((0, 0), (0, 128 - GLA_RANK))).astype(BF16)
        w_up = jnp.pad(gla_w_gate_up[l], ((0, 128 - GLA_RANK), (0, 0))).astype(BF16)
        qn = jnp.tile(diff_q_norm[l], 2 * H_D).reshape(1, -1)
        kn = jnp.tile(diff_k_norm[l], 2 * H_D).reshape(1, -1)
        lam4 = jnp.stack([diff_lam_q1[l], diff_lam_k1[l], diff_lam_q2[l], diff_lam_k2[l]]).astype(F32)
        wo_mix = w_out[l].astype(BF16)
        wq_x, wk_x, wv_x, wo_x = (w[l].astype(BF16) for w in (xattn_wq, xattn_wk, xattn_wv, xattn_wo))

        tab = rel_bias.astype(F32).T.reshape(-1)
        bias_tiles = _bias_prompt_tiles(tab, T=T)
        bias_paged = _bias_paged_tiles(tab, LQ=Ld)

        xp = _ffn(xp, row(ffn1_norm), w1g, w1u, w1d, tm=1024, tf=256)
        xs = _ffn(xs, row(ffn1_norm), w1g, w1u, w1d, tm=1024, tf=256)

        mix_args = (row(mix_norm), w_main, w_lr, w_up, row(gla_b_gate), qn, kn, g64)
        qg, kg, vg, rg, lf, qd, kdt, kdb, vd, vdb = _mixin(xp, *mix_args, tm=TM, seq=L)
        qg_s, kg_s, vg_s, rg_s, lf_s, qd_s, kd_s, kdb_s, vd_s, vdb_s = _mixin(xs, *mix_args, tm=TM)
        b3 = lambda a: a.reshape(B, L, -1)
        s3 = lambda a: a.reshape(Bd, Ld, -1)
        og_s, ss = _gla(s3(qg_s), s3(kg_s), s3(vg_s), s3(lf_s), state_gla[l], tb=Ld, G=8)

        kt_pool = jnp.transpose(cache_diff_k[l], (0, 2, 3, 4, 1)).reshape(n_pool, H_D * 2 * DK_D, PAGE)
        v_pool = cache_diff_v[l].reshape(n_pool, PAGE * H_D, DV_D)
        s0 = jnp.zeros((B, H_G, DK_G, DV_G), F32)
        od_s, og, sp = _paged_attn_with_gla(s3(qd_s), s3(kdb_s), s3(vdb_s), kt_pool, v_pool, page_table,
                                            bias_paged, b3(qg), b3(kg), b3(vg), b3(lf), s0, P=P_PAGES)

        od = _dattn_prompt(b3(qd), b3(kdb), b3(vdb), bias_tiles, T=T)
        dh = D // H_X
        mem2 = mem_prompt.reshape(B * M, D)
        mk, mkb = _proj(mem2, row(mem_norm), wk_x, row(xattn_k_norm), tm=TM, head_norm=True, scale=1.0,
                        out_dtypes=(F32, BF16))
        mv, mvb = _proj(mem2, row(mem_norm), wv_x, jnp.ones((1, dh), F32), tm=TM, head_norm=False,
                        scale=1.0, out_dtypes=(F32, BF16))
        xp = _mix_xattn(xp, og.reshape(B * L, -1), rg, od.reshape(B * L, -1), lam4, row(gla_out_norm),
                        row(diff_subln), wo_mix, row(xattn_norm), wq_x, row(xattn_q_norm),
                        mkb.reshape(B, M, D), mvb.reshape(B, M, D), wo_x, tm=TM, seq=L, lam_init=lam_init)
        outs["mk"].append(mk.reshape(B, M, H_X, dh))
        outs["mv"].append(mv.reshape(B, M, H_X, dh))
        outs["kp"].append(kdt.reshape(B, H_D, 2, DK_D, L).transpose(0, 4, 1, 2, 3))
        outs["vp"].append(vd.reshape(B, L, H_D, DV_D))
        outs["sp"].append(sp)

        od_s = od_s.reshape(Bd, H_D, 2, Ld, DV_D).transpose(0, 3, 1, 2, 4).reshape(Bd * Ld, -1)
        xs = _mixout(xs, og_s.reshape(Bd * Ld, -1), rg_s, od_s, lam4, row(gla_out_norm), row(diff_subln),
                     wo_mix, tm=TM, lam_init=lam_init)
        outs["ks"].append(kd_s.reshape(Bd, Ld, H_D, 2, DK_D))
        outs["vs"].append(vd_s.reshape(Bd, Ld, H_D, DV_D))
        outs["ss"].append(ss)

        nlb = dh // LANES
        (qs,) = _proj(xs, row(xattn_norm), wq_x, row(xattn_q_norm), tm=TM, head_norm=True,
                      scale=dh ** -0.5, out_dtypes=(BF16,))
        cache_rows = lambda c: (c[l].reshape(Bd, M, H_X, nlb, LANES).transpose(0, 1, 3, 2, 4)
                                .reshape(Bd, M * nlb * H_X, LANES))
        osm = _xattn_cached(qs.reshape(Bd, Ld, D), cache_rows(cache_mem_k), cache_rows(cache_mem_v), M=M,
                            G=8)
        xs = _resproj(xs, osm.reshape(Bd * Ld, D), wo_x, tm=TM)

        xp = _ffn(xp, row(ffn2_norm), w2g, w2u, w2d, tm=1024, tf=256)
        xs = _ffn(xs, row(ffn2_norm), w2g, w2u, w2d, tm=1024, tf=256)

    st = lambda key: jnp.stack(outs[key])
    return (xp.reshape(B, L, D), xs.reshape(Bd, Ld, D), st("kp"), st("vp"), st("ks"), st("vs"),
            st("sp"), st("ss"), st("mk"), st("mv"))
```
